```python
import math
import jax, jax.numpy as jnp
from jax import lax
import numpy as np

D_MODEL = 1024
BATCH = 4
SEQ = 8192
DEPTH = 2

HEAD_DIM = 64
A_HEADS = D_MODEL // 256
A_VDIM = 2 * HEAD_DIM
A_QK_WIDTH = A_HEADS * 2 * HEAD_DIM
A_V_WIDTH = A_HEADS * A_VDIM
B_HEADS = D_MODEL // 128
B_WIDTH = B_HEADS * HEAD_DIM
IN_WIDTH = 2 * A_QK_WIDTH + A_V_WIDTH + 3 * B_WIDTH
MIX_WIDTH = A_V_WIDTH + B_WIDTH
D_FF = 2816
GRID_W = 64
NA_ROWS = 8
NA_COLS = 16
POOL_WINDOWS = (2, 4, 8, 16)
POOL_GROUPS = 4
POOL_GROUP_DIM = D_MODEL // POOL_GROUPS
ROPE_THETA = 10000.0
Q_BLOCK = 128
EPS = 1e-6
N_MOD = 9
N_EVEN = (DEPTH + 1) // 2
N_ODD = DEPTH // 2
NEG_INF = -1e30

kernel_name = "hybrid_diffattn_natten_pool_macaron_adaln"


def rmsnorm(x, g):
    xf = x.astype(jnp.float32)
    y = xf * lax.rsqrt(jnp.mean(xf * xf, axis=-1, keepdims=True) + EPS)
    return (y * g.astype(jnp.float32)).astype(x.dtype)


def modulate(h, shift, scale):
    return h * (1 + scale[:, None, :]) + shift[:, None, :]


def swiglu(h, w13, w2):
    a, b = jnp.split(h @ w13, 2, axis=-1)
    return (jax.nn.silu(a) * b) @ w2


def rope(x, pos):
    d = x.shape[-1]
    inv = ROPE_THETA ** (-jnp.arange(0, d, 2, dtype=jnp.float32) / d)
    ang = pos.astype(jnp.float32)[:, None] * inv[None, :]
    cos = jnp.cos(ang).astype(x.dtype)
    sin = jnp.sin(ang).astype(x.dtype)
    x1, x2 = x[..., : d // 2], x[..., d // 2:]
    return jnp.concatenate([x1 * cos - x2 * sin, x1 * sin + x2 * cos], axis=-1)


def diff_attention(q, k, v, qk_norm, lam_params, subln_g, lambda_init):
    B_, S, H, _, d = q.shape
    pos = jnp.arange(S)
    q = rope(rmsnorm(q, qk_norm[0]).transpose(0, 2, 3, 1, 4), pos)
    k = rope(rmsnorm(k, qk_norm[1]).transpose(0, 2, 3, 1, 4), pos)
    v = v.transpose(0, 2, 1, 3)
    lp = lam_params.astype(jnp.float32)
    lam = jnp.exp(jnp.sum(lp[0] * lp[1])) - jnp.exp(jnp.sum(lp[2] * lp[3])) + lambda_init
    scale = d ** -0.5
    nb = S // Q_BLOCK
    qb = q.reshape(B_, H, 2, nb, Q_BLOCK, d).transpose(3, 0, 1, 2, 4, 5)

    def block(qblk):
        s = jnp.einsum('bhmqd,bhmkd->bhmqk', qblk, k).astype(jnp.float32) * scale
        p = jax.nn.softmax(s, axis=-1)
        w = p[:, :, 0] - lam * p[:, :, 1]
        return jnp.einsum('bhqk,bhkv->bhqv', w.astype(v.dtype), v)

    o = lax.map(block, qb)
    o = o.transpose(1, 0, 3, 2, 4).reshape(B_, S, H, 2 * d)
    o = rmsnorm(o, subln_g) * (1.0 - lambda_init)
    return o.reshape(B_, S, H * 2 * d)


def neighbourhood_attention(q, k, v, qk_norm, rpb):
    B_, S, H, d = q.shape
    rows = S // GRID_W
    kr = min(NA_ROWS, rows)
    q = rmsnorm(q, qk_norm[0])
    k = rmsnorm(k, qk_norm[1])

    def to_grid(t):
        return t.reshape(B_, rows, GRID_W, H, d).transpose(1, 0, 3, 2, 4)

    qg, kg, vg = to_grid(q), to_grid(k), to_grid(v)
    col = jnp.arange(GRID_W)
    col_start = jnp.clip(col - NA_COLS // 2, 0, GRID_W - NA_COLS)
    col_in = (col[None, :] >= col_start[:, None]) & (col[None, :] < col_start[:, None] + NA_COLS)
    dc_idx = jnp.clip(col[None, :] - col[:, None] + NA_COLS - 1, 0, 2 * NA_COLS - 2)
    scale = d ** -0.5

    def row_block(args):
        qr, r = args
        rs = jnp.clip(r - kr // 2, 0, rows - kr)
        kb = lax.dynamic_slice_in_dim(kg, rs, kr, axis=0)
        vb = lax.dynamic_slice_in_dim(vg, rs, kr, axis=0)
        s = jnp.einsum('bhqd,rbhkd->bhqrk', qr, kb).astype(jnp.float32) * scale
        dr_idx = rs + jnp.arange(kr) - r + NA_ROWS - 1
        bias = rpb[:, dr_idx][:, :, dc_idx].transpose(0, 2, 1, 3)
        s = s + bias[None].astype(jnp.float32)
        s = jnp.where(col_in[:, None, :], s, NEG_INF)
        p = jax.nn.softmax(s.reshape(B_, H, GRID_W, kr * GRID_W), axis=-1)
        p = p.reshape(B_, H, GRID_W, kr, GRID_W).astype(vb.dtype)
        return jnp.einsum('bhqrk,rbhkd->bhqd', p, vb)

    o = lax.map(row_block, (qg, jnp.arange(rows)))
    return o.transpose(1, 0, 3, 2, 4).reshape(B_, S, H * d)


def even_mixer(h, w_in, w_out, a_qk_norm, a_lambda, a_subln, b_qk_norm, b_rpb, lambda_init):
    B_, S, _ = h.shape
    u = h @ w_in
    qa, ka, va, qb, kb, vb = jnp.split(u, 6, axis=-1)
    oa = diff_attention(qa.reshape(B_, S, A_HEADS, 2, HEAD_DIM),
                        ka.reshape(B_, S, A_HEADS, 2, HEAD_DIM),
                        va.reshape(B_, S, A_HEADS, A_VDIM),
                        a_qk_norm, a_lambda, a_subln, lambda_init)
    ob = neighbourhood_attention(qb.reshape(B_, S, B_HEADS, HEAD_DIM),
                                 kb.reshape(B_, S, B_HEADS, HEAD_DIM),
                                 vb.reshape(B_, S, B_HEADS, HEAD_DIM),
                                 b_qk_norm, b_rpb)
    return jnp.concatenate([oa, ob], axis=-1) @ w_out


def multiscale_pool(h, w_grp, b_grp, scale):
    B_, S, D = h.shape
    hf = h.astype(jnp.float32)
    cs = jnp.concatenate([jnp.zeros((B_, 1, D), jnp.float32), jnp.cumsum(hf, axis=1)], axis=1)
    t = jnp.arange(S)
    outs = []
    for g, w in enumerate(POOL_WINDOWS):
        lo = jnp.clip(t - w // 2, 0, S)
        hi = jnp.clip(t - w // 2 + w, 0, S)
        csg = cs[:, :, g * POOL_GROUP_DIM:(g + 1) * POOL_GROUP_DIM]
        seg = jnp.take(csg, hi, axis=1) - jnp.take(csg, lo, axis=1)
        outs.append(seg / (hi - lo).astype(jnp.float32)[None, :, None])
    pooled = jnp.concatenate(outs, axis=-1)
    y = (pooled - hf).astype(h.dtype).reshape(B_, S, POOL_GROUPS, POOL_GROUP_DIM)
    y = jnp.einsum('bsgi,gio->bsgo', y, w_grp).reshape(B_, S, D) + b_grp
    return y * scale


def setup_inputs(seed: int = 0) -> dict:
    key = jax.random.key(seed)
    ks = jax.random.split(key, 17)

    def nrm(k, shape, s):
        return jax.random.normal(k, shape, jnp.float32) * s

    return {
        "x": nrm(ks[0], (BATCH, SEQ, D_MODEL), 1.0),
        "c": nrm(ks[1], (BATCH, D_MODEL), 1.0),
        "ada_w": nrm(ks[2], (DEPTH, D_MODEL, N_MOD * D_MODEL), 0.02),
        "ada_b": nrm(ks[3], (DEPTH, N_MOD * D_MODEL), 0.02),
        "norm_g": 1.0 + nrm(ks[4], (DEPTH, 3, D_MODEL), 0.02),
        "ff_w13": nrm(ks[5], (DEPTH, 2, D_MODEL, 2 * D_FF), D_MODEL ** -0.5),
        "ff_w2": nrm(ks[6], (DEPTH, 2, D_FF, D_MODEL), D_FF ** -0.5),
        "w_in": nrm(ks[7], (N_EVEN, D_MODEL, IN_WIDTH), D_MODEL ** -0.5),
        "w_out": nrm(ks[8], (N_EVEN, MIX_WIDTH, D_MODEL), MIX_WIDTH ** -0.5),
        "a_qk_norm": 1.0 + nrm(ks[9], (N_EVEN, 2, HEAD_DIM), 0.02),
        "a_lambda": nrm(ks[10], (N_EVEN, 4, HEAD_DIM), 0.1),
        "a_subln": 1.0 + nrm(ks[11], (N_EVEN, A_VDIM), 0.02),
        "b_qk_norm": 1.0 + nrm(ks[12], (N_EVEN, 2, HEAD_DIM), 0.02),
        "b_rpb": nrm(ks[13], (N_EVEN, B_HEADS, 2 * NA_ROWS - 1, 2 * NA_COLS - 1), 0.1),
        "pool_w": nrm(ks[14], (N_ODD, POOL_GROUPS, POOL_GROUP_DIM, POOL_GROUP_DIM), POOL_GROUP_DIM ** -0.5),
        "pool_b": nrm(ks[15], (N_ODD, D_MODEL), 0.02),
        "pool_scale": 1.0 + nrm(ks[16], (N_ODD, D_MODEL), 0.1),
    }


def reference(x, c, ada_w, ada_b, norm_g, ff_w13, ff_w2, w_in, w_out, a_qk_norm, a_lambda,
              a_subln, b_qk_norm, b_rpb, pool_w, pool_b, pool_scale):
    B_, S, D = x.shape
    cond = jax.nn.silu(c)
    for layer in range(DEPTH):
        mod = (cond @ ada_w[layer] + ada_b[layer]).reshape(B_, N_MOD, D)
        sh1, sc1, g1, sh2, sc2, g2, sh3, sc3, g3 = [mod[:, i] for i in range(N_MOD)]
        h = modulate(rmsnorm(x, norm_g[layer, 0]), sh1, sc1)
        x = x + 0.5 * g1[:, None, :] * swiglu(h, ff_w13[layer, 0], ff_w2[layer, 0])
        h = modulate(rmsnorm(x, norm_g[layer, 1]), sh2, sc2)
        if layer % 2 == 0:
            e = layer // 2
            lambda_init = 0.8 - 0.6 * math.exp(-0.3 * layer)
            mix = even_mixer(h, w_in[e], w_out[e], a_qk_norm[e], a_lambda[e], a_subln[e],
                             b_qk_norm[e], b_rpb[e], lambda_init)
        else:
            o = layer // 2
            mix = multiscale_pool(h, pool_w[o], pool_b[o], pool_scale[o])
        x = x + g2[:, None, :] * mix
        h = modulate(rmsnorm(x, norm_g[layer, 2]), sh3, sc3)
        x = x + 0.5 * g3[:, None, :] * swiglu(h, ff_w13[layer, 1], ff_w2[layer, 1])
    return x
```

```python
import functools
import math

import numpy as np
import jax
import jax.numpy as jnp
from jax import lax
from jax.experimental import pallas as pl
from jax.experimental.pallas import tpu as pltpu

F32 = jnp.float32
BF16 = jnp.bfloat16

D_MODEL = 1024
HEAD_DIM = 64
A_HEADS = 4
A_VDIM = 2 * HEAD_DIM
B_HEADS = 8
D_FF = 2816
GRID_W = 64
NA_ROWS = 8
NA_COLS = 16
POOL_WINDOWS = (2, 4, 8, 16)
POOL_GROUP_DIM = D_MODEL // len(POOL_WINDOWS)
ROPE_THETA = 10000.0
EPS = 1e-6
N_MOD = 9
NEG_INF = -1e30
QK_SCALE = HEAD_DIM ** -0.5

V7X_VMEM_LIMIT_BYTES = 56 * 1024 * 1024
FF_CHUNK = 256
NA_BLOCK_ROWS = 4
NA_WIN_ROWS = 3 * NA_BLOCK_ROWS
POOL_HALO = 8


def _params(*sem):
    return pltpu.CompilerParams(dimension_semantics=sem, vmem_limit_bytes=V7X_VMEM_LIMIT_BYTES)


def _resident(shape, index_map):
    return pl.BlockSpec(shape, index_map, pipeline_mode=pl.Buffered(1))


def _norm_mod(x, g, shift, scale):
    ms = jnp.mean(x * x, axis=-1, keepdims=True)
    y = (x * lax.rsqrt(ms + EPS)) * g
    return y * (1.0 + scale) + shift


def _adaln_kernel(c_ref, w_ref, b_ref, o_ref):
    c = c_ref[...]
    cond = c / (1.0 + jnp.exp(-c))
    o_ref[...] = jnp.dot(cond, w_ref[...], preferred_element_type=F32,
                         precision=lax.Precision.HIGHEST) + b_ref[...]


def _adaln(c, ada_w, ada_b):
    depth, d, n = ada_w.shape
    b = c.shape[0]
    bp = -(-b // 8) * 8
    cp = jnp.pad(c, ((0, bp - b), (0, 0)))
    tn = 1024
    out = pl.pallas_call(
        _adaln_kernel,
        grid=(depth, n // tn),
        in_specs=[
            pl.BlockSpec((bp, d), lambda l, j: (0, 0)),
            pl.BlockSpec((None, d, tn), lambda l, j: (l, 0, j)),
            pl.BlockSpec((None, 1, tn), lambda l, j: (l, 0, j)),
        ],
        out_specs=pl.BlockSpec((None, bp, tn), lambda l, j: (l, 0, j)),
        out_shape=jax.ShapeDtypeStruct((depth, bp, n), F32),
        compiler_params=_params("arbitrary", "arbitrary"),
        name="adaln",
    )(cp, ada_w, ada_b.reshape(depth, 1, n))
    return out[:, :b].reshape(depth, b, N_MOD, d)


def _ffn_kernel(x_ref, mod_ref, g_ref, w13_ref, w2_ref, o_ref, act_ref, *, row0):
    x = x_ref[...]
    h = _norm_mod(x, g_ref[...], mod_ref[row0:row0 + 1, :], mod_ref[row0 + 1:row0 + 2, :]).astype(BF16)
    for ci in range(D_FF // FF_CHUNK):
        lo = ci * FF_CHUNK
        a = jnp.dot(h, w13_ref[:, lo:lo + FF_CHUNK], preferred_element_type=F32)
        b = jnp.dot(h, w13_ref[:, D_FF + lo:D_FF + lo + FF_CHUNK], preferred_element_type=F32)
        act_ref[:, lo:lo + FF_CHUNK] = ((a / (1.0 + jnp.exp(-a))) * b).astype(BF16)
    y = jnp.dot(act_ref[...], w2_ref[...], preferred_element_type=F32)
    o_ref[...] = x + (0.5 * mod_ref[row0 + 2:row0 + 3, :]) * y


def _ffn(x, mod, g, w13, w2, row0, tm):
    b, s, d = x.shape
    return pl.pallas_call(
        functools.partial(_ffn_kernel, row0=row0),
        grid=(b, s // tm),
        in_specs=[
            pl.BlockSpec((None, tm, d), lambda bi, i: (bi, i, 0)),
            pl.BlockSpec((None, N_MOD, d), lambda bi, i: (bi, 0, 0)),
            pl.BlockSpec((1, d), lambda bi, i: (0, 0)),
            _resident((d, 2 * D_FF), lambda bi, i: (0, 0)),
            _resident((D_FF, d), lambda bi, i: (0, 0)),
        ],
        out_specs=pl.BlockSpec((None, tm, d), lambda bi, i: (bi, i, 0)),
        out_shape=jax.ShapeDtypeStruct(x.shape, F32),
        scratch_shapes=[pltpu.VMEM((tm, D_FF), BF16)],
        compiler_params=_params("parallel", "parallel"),
        name="ffn",
    )(x, mod, g.reshape(1, d), w13, w2)


def _proj_kernel(x_ref, mod_ref, g_ref, wt_ref, gain_ref, cos_ref, sin_ref,
                 qa_ref, ka_ref, va_ref, qb_ref, kb_ref, vb_ref):
    tm = x_ref.shape[0]
    h = _norm_mod(x_ref[...], g_ref[...], mod_ref[3:4, :], mod_ref[4:5, :]).astype(BF16)
    cos = cos_ref[...]
    sin = sin_ref[...]
    width = B_HEADS * HEAD_DIM

    def group_t(gi):
        return lax.dot_general(wt_ref[gi * width:(gi + 1) * width, :], h,
                               (((1,), (1,)), ((), ())), preferred_element_type=F32)

    def head_norm(ut, gain):
        x3 = ut.reshape(B_HEADS, HEAD_DIM, tm)
        ms = jnp.mean(x3 * x3, axis=1, keepdims=True)
        return (x3 * lax.rsqrt(ms + EPS)) * gain[None]

    def rope(x3):
        x1 = x3[:, :HEAD_DIM // 2, :]
        x2 = x3[:, HEAD_DIM // 2:, :]
        return jnp.concatenate([x1 * cos - x2 * sin, x1 * sin + x2 * cos], axis=1)

    qa = rope(head_norm(group_t(0), gain_ref[0])) * QK_SCALE
    qa_ref[...] = qa.reshape(A_HEADS, A_VDIM, tm).astype(BF16)
    ka = rope(head_norm(group_t(1), gain_ref[1])).reshape(A_HEADS, A_VDIM, tm)
    for hh in range(A_HEADS):
        ka_ref[hh] = ka[hh].T.astype(BF16)
    va_ref[...] = group_t(2).reshape(A_HEADS, A_VDIM, tm).astype(BF16)
    qb = head_norm(group_t(3), gain_ref[2]) * QK_SCALE
    qb_ref[...] = qb.astype(BF16)
    kb = head_norm(group_t(4), gain_ref[3]).reshape(B_HEADS // 2, 2 * HEAD_DIM, tm)
    for hp in range(B_HEADS // 2):
        kb_ref[hp] = kb[hp].T.astype(BF16)
    vb_ref[...] = group_t(5).reshape(B_HEADS, HEAD_DIM, tm).astype(BF16)


def _proj(x, mod, g, w_in_t, gains, cos_t, sin_t, tm):
    b, s, d = x.shape
    tok = lambda bi, i: (bi, 0, 0, i)
    seq = lambda bi, i: (bi, 0, i, 0)
    return pl.pallas_call(
        _proj_kernel,
        grid=(b, s // tm),
        in_specs=[
            pl.BlockSpec((None, tm, d), lambda bi, i: (bi, i, 0)),
            pl.BlockSpec((None, N_MOD, d), lambda bi, i: (bi, 0, 0)),
            pl.BlockSpec((1, d), lambda bi, i: (0, 0)),
            _resident(w_in_t.shape, lambda bi, i: (0, 0)),
            pl.BlockSpec((4, HEAD_DIM, tm), lambda bi, i: (0, 0, 0)),
            pl.BlockSpec((HEAD_DIM // 2, tm), lambda bi, i: (0, i)),
            pl.BlockSpec((HEAD_DIM // 2, tm), lambda bi, i: (0, i)),
        ],
        out_specs=[
            pl.BlockSpec((None, A_HEADS, A_VDIM, tm), tok),
            pl.BlockSpec((None, A_HEADS, tm, A_VDIM), seq),
            pl.BlockSpec((None, A_HEADS, A_VDIM, tm), tok),
            pl.BlockSpec((None, B_HEADS, HEAD_DIM, tm), tok),
            pl.BlockSpec((None, B_HEADS // 2, tm, 2 * HEAD_DIM), seq),
            pl.BlockSpec((None, B_HEADS, HEAD_DIM, tm), tok),
        ],
        out_shape=[
            jax.ShapeDtypeStruct((b, A_HEADS, A_VDIM, s), BF16),
            jax.ShapeDtypeStruct((b, A_HEADS, s, A_VDIM), BF16),
            jax.ShapeDtypeStruct((b, A_HEADS, A_VDIM, s), BF16),
            jax.ShapeDtypeStruct((b, B_HEADS, HEAD_DIM, s), BF16),
            jax.ShapeDtypeStruct((b, B_HEADS // 2, s, 2 * HEAD_DIM), BF16),
            jax.ShapeDtypeStruct((b, B_HEADS, HEAD_DIM, s), BF16),
        ],
        compiler_params=_params("parallel", "parallel"),
        name="qkv_proj",
    )(x, mod, g.reshape(1, d), w_in_t, gains, cos_t, sin_t)


def _dattn_kernel(q_ref, k_ref, v_ref, lam_ref, sg_ref, o_ref, *, kblk, lambda_init):
    s_len = k_ref.shape[0]
    qn = q_ref.shape[1]
    qt = q_ref[...]
    z = jnp.zeros((HEAD_DIM, qn), BF16)
    rhs = jnp.concatenate([jnp.concatenate([qt[:HEAD_DIM], z], axis=0),
                           jnp.concatenate([z, qt[HEAD_DIM:]], axis=0)], axis=1)

    def body(j, carry):
        m, l, acc = carry
        k0 = pl.multiple_of(j * kblk, kblk)
        st = jnp.dot(k_ref[pl.ds(k0, kblk), :], rhs, preferred_element_type=F32)
        m_new = jnp.maximum(m, jnp.max(st, axis=0, keepdims=True))
        alpha = jnp.exp(m - m_new)
        p = jnp.exp(st - m_new)
        l = alpha * l + jnp.sum(p, axis=0, keepdims=True)
        pv = jnp.dot(v_ref[:, pl.ds(k0, kblk)], p.astype(BF16), preferred_element_type=F32)
        return m_new, l, acc * alpha + pv

    init = (jnp.full((1, 2 * qn), NEG_INF, F32), jnp.zeros((1, 2 * qn), F32),
            jnp.zeros((A_VDIM, 2 * qn), F32))
    _, l, acc = lax.fori_loop(0, s_len // kblk, body, init)
    o = acc / l
    lp = lam_ref[...]
    lam = (jnp.exp(jnp.sum(lp[0:1] * lp[1:2], axis=-1, keepdims=True))
           - jnp.exp(jnp.sum(lp[2:3] * lp[3:4], axis=-1, keepdims=True)) + lambda_init)
    ot = o[:, :qn] - lam * o[:, qn:]
    ms = jnp.mean(ot * ot, axis=0, keepdims=True)
    y = ((ot * lax.rsqrt(ms + EPS)) * sg_ref[...]) * (1.0 - lambda_init)
    o_ref[...] = y.T.astype(BF16)


def _dattn(qa_t, ka, va_t, a_lambda, a_subln, lambda_init, qblk, kblk):
    b, nh, _, s = qa_t.shape
    return pl.pallas_call(
        functools.partial(_dattn_kernel, kblk=kblk, lambda_init=lambda_init),
        grid=(b, nh, s // qblk),
        in_specs=[
            pl.BlockSpec((None, None, A_VDIM, qblk), lambda bi, h, i: (bi, h, 0, i)),
            pl.BlockSpec((None, None, s, A_VDIM), lambda bi, h, i: (bi, h, 0, 0)),
            pl.BlockSpec((None, None, A_VDIM, s), lambda bi, h, i: (bi, h, 0, 0)),
            pl.BlockSpec((4, HEAD_DIM), lambda bi, h, i: (0, 0)),
            pl.BlockSpec((A_VDIM, qblk), lambda bi, h, i: (0, 0)),
        ],
        out_specs=pl.BlockSpec((None, qblk, A_VDIM), lambda bi, h, i: (bi, i, h)),
        out_shape=jax.ShapeDtypeStruct((b, s, nh * A_VDIM), BF16),
        compiler_params=_params("parallel", "parallel", "arbitrary"),
        name="diff_attn",
    )(qa_t, ka, va_t, a_lambda, jnp.broadcast_to(a_subln[:, None], (A_VDIM, qblk)))


def _na_bias_table(rpb):
    i = np.arange(NA_WIN_ROWS)[:, None]
    j = np.arange(NA_BLOCK_ROWS)[None, :]
    half = NA_ROWS // 2
    dr = np.stack([i - j + NA_ROWS - 1, i - j + half - 1, i - j - 1])
    row_ok = np.stack([np.broadcast_to(i < NA_ROWS, dr[0].shape),
                       (i >= j) & (i < j + NA_ROWS),
                       np.broadcast_to(i >= NA_WIN_ROWS - NA_ROWS, dr[0].shape)])
    dr = np.clip(dr, 0, 2 * NA_ROWS - 2)
    kc = np.arange(GRID_W)[:, None]
    qc = np.arange(GRID_W)[None, :]
    cs = np.clip(qc - NA_COLS // 2, 0, GRID_W - NA_COLS)
    col_ok = (kc >= cs) & (kc < cs + NA_COLS)
    dc = np.clip(kc - qc + NA_COLS - 1, 0, 2 * NA_COLS - 2)
    tab = rpb[:, dr[:, :, :, None, None], dc[None, None, None, :, :]]
    ok = row_ok[:, :, :, None, None] & col_ok[None, None, None, :, :]
    tab = jnp.where(ok[None], tab.astype(F32), NEG_INF)
    tab = tab.transpose(1, 0, 2, 4, 3, 5)
    return tab.reshape(3, rpb.shape[0], NA_WIN_ROWS * GRID_W, NA_BLOCK_ROWS * GRID_W)


def _na_kernel(q_ref, k_ref, v_ref, bias_ref, o_ref, *, nblk):
    qn = NA_BLOCK_ROWS * GRID_W
    win = NA_WIN_ROWS * GRID_W
    rb = pl.program_id(2)
    t0 = pl.multiple_of(jnp.clip(rb - 1, 0, nblk - 3) * qn, qn)
    kwin = k_ref[pl.ds(t0, win), :]
    z = jnp.zeros((HEAD_DIM, qn), BF16)
    outs = []
    for hh in range(2):
        qt = q_ref[hh]
        rhs = jnp.concatenate([qt, z] if hh == 0 else [z, qt], axis=0)
        st = jnp.dot(kwin, rhs, preferred_element_type=F32) + bias_ref[hh]
        m = jnp.max(st, axis=0, keepdims=True)
        p = jnp.exp(st - m)
        l = jnp.sum(p, axis=0, keepdims=True)
        pv = jnp.dot(v_ref[hh, :, pl.ds(t0, win)], p.astype(BF16), preferred_element_type=F32)
        outs.append(pv / l)
    o_ref[...] = jnp.concatenate(outs, axis=0).T.astype(BF16)


def _na(qb_t, kb, vb_t, bias_tab):
    b, nh, _, s = qb_t.shape
    qn = NA_BLOCK_ROWS * GRID_W
    win = NA_WIN_ROWS * GRID_W
    nblk = s // qn
    assert nblk >= 3

    def bias_idx(bi, hp, rb):
        kind = jnp.where(rb == 0, 0, jnp.where(rb == nblk - 1, 2, 1))
        return (kind, hp, 0, 0)

    return pl.pallas_call(
        functools.partial(_na_kernel, nblk=nblk),
        grid=(b, nh // 2, nblk),
        in_specs=[
            pl.BlockSpec((None, 2, HEAD_DIM, qn), lambda bi, hp, rb: (bi, hp, 0, rb)),
            pl.BlockSpec((None, None, s, 2 * HEAD_DIM), lambda bi, hp, rb: (bi, hp, 0, 0)),
            pl.BlockSpec((None, 2, HEAD_DIM, s), lambda bi, hp, rb: (bi, hp, 0, 0)),
            pl.BlockSpec((None, 2, win, qn), bias_idx),
        ],
        out_specs=pl.BlockSpec((None, qn, 2 * HEAD_DIM), lambda bi, hp, rb: (bi, rb, hp)),
        out_shape=jax.ShapeDtypeStruct((b, s, nh * HEAD_DIM), BF16),
        compiler_params=_params("parallel", "parallel", "arbitrary"),
        name="nbr_attn",
    )(qb_t, kb, vb_t, bias_tab)


def _outproj_kernel(x_ref, ma_ref, mb_ref, mod_ref, w_ref, o_ref):
    half = ma_ref.shape[1]
    y = (jnp.dot(ma_ref[...], w_ref[:half, :], preferred_element_type=F32)
         + jnp.dot(mb_ref[...], w_ref[half:, :], preferred_element_type=F32))
    o_ref[...] = x_ref[...] + mod_ref[5:6, :] * y


def _outproj(x, mix_a, mix_b, mod, w_out, tm):
    b, s, d = x.shape
    half = mix_a.shape[-1]
    return pl.pallas_call(
        _outproj_kernel,
        grid=(b, s // tm),
        in_specs=[
            pl.BlockSpec((None, tm, d), lambda bi, i: (bi, i, 0)),
            pl.BlockSpec((None, tm, half), lambda bi, i: (bi, i, 0)),
            pl.BlockSpec((None, tm, half), lambda bi, i: (bi, i, 0)),
            pl.BlockSpec((None, N_MOD, d), lambda bi, i: (bi, 0, 0)),
            _resident(w_out.shape, lambda bi, i: (0, 0)),
        ],
        out_specs=pl.BlockSpec((None, tm, d), lambda bi, i: (bi, i, 0)),
        out_shape=jax.ShapeDtypeStruct(x.shape, F32),
        compiler_params=_params("parallel", "parallel"),
        name="mix_out_proj",
    )(x, mix_a, mix_b, mod, w_out)


def _pool_kernel(x_ref, xp_ref, xn_ref, mod_ref, g_ref, w_ref, b_ref, sc_ref, o_ref, *, s_len):
    tm = x_ref.shape[0]
    i = pl.program_id(1)
    g = g_ref[...]
    shift = mod_ref[3:4, :]
    scale = mod_ref[4:5, :]
    x = x_ref[...]
    h = _norm_mod(x, g, shift, scale)
    hp = _norm_mod(xp_ref[...], g, shift, scale) * (i > 0).astype(F32)
    hn = _norm_mod(xn_ref[...], g, shift, scale) * (i < pl.num_programs(1) - 1).astype(F32)
    hext = jnp.concatenate([hp, h, hn], axis=0)
    t = i * tm + lax.broadcasted_iota(jnp.int32, (tm, 1), 0)
    outs = []
    for gi, w in enumerate(POOL_WINDOWS):
        lo = gi * POOL_GROUP_DIM
        run = hext[:, lo:lo + POOL_GROUP_DIM]
        span = 1
        while span < w:
            n = run.shape[0]
            run = run[:n - span] + run[span:]
            span *= 2
        start = POOL_HALO - w // 2
        seg = run[start:start + tm]
        cnt = (jnp.minimum(t + w // 2, s_len) - jnp.maximum(t - w // 2, 0)).astype(F32)
        y = seg / cnt - h[:, lo:lo + POOL_GROUP_DIM]
        outs.append(jnp.dot(y.astype(BF16), w_ref[gi], preferred_element_type=F32))
    y = (jnp.concatenate(outs, axis=-1) + b_ref[...]) * sc_ref[...]
    o_ref[...] = x + mod_ref[5:6, :] * y


def _pool(x, mod, g, pool_w, pool_b, pool_scale, tm):
    b, s, d = x.shape
    per = tm // POOL_HALO
    last = s // POOL_HALO - 1
    return pl.pallas_call(
        functools.partial(_pool_kernel, s_len=s),
        grid=(b, s // tm),
        in_specs=[
            pl.BlockSpec((None, tm, d), lambda bi, i: (bi, i, 0)),
            pl.BlockSpec((None, POOL_HALO, d), lambda bi, i: (bi, jnp.maximum(i * per - 1, 0), 0)),
            pl.BlockSpec((None, POOL_HALO, d), lambda bi, i: (bi, jnp.minimum((i + 1) * per, last), 0)),
            pl.BlockSpec((None, N_MOD, d), lambda bi, i: (bi, 0, 0)),
            pl.BlockSpec((1, d), lambda bi, i: (0, 0)),
            pl.BlockSpec(pool_w.shape, lambda bi, i: (0, 0, 0)),
            pl.BlockSpec((1, d), lambda bi, i: (0, 0)),
            pl.BlockSpec((1, d), lambda bi, i: (0, 0)),
        ],
        out_specs=pl.BlockSpec((None, tm, d), lambda bi, i: (bi, i, 0)),
        out_shape=jax.ShapeDtypeStruct(x.shape, F32),
        compiler_params=_params("parallel", "parallel"),
        name="pool_mixer",
    )(x, x, x, mod, g.reshape(1, d), pool_w, pool_b.reshape(1, d), pool_scale.reshape(1, d))


def _tile(s, want):
    return want if s % want == 0 else s


def kernel(x, c, ada_w, ada_b, norm_g, ff_w13, ff_w2, w_in, w_out, a_qk_norm, a_lambda, a_subln,
           b_qk_norm, b_rpb, pool_w, pool_b, pool_scale):
    b, s, d = x.shape
    depth = ada_w.shape[0]
    tm = _tile(s, 512)
    mod = _adaln(c, ada_w, ada_b)

    pos = jnp.arange(s, dtype=F32)
    inv = ROPE_THETA ** (-jnp.arange(0, HEAD_DIM, 2, dtype=F32) / HEAD_DIM)
    ang = inv[:, None] * pos[None, :]
    cos_t, sin_t = jnp.cos(ang), jnp.sin(ang)

    for layer in range(depth):
        m = mod[layer]
        x = _ffn(x, m, norm_g[layer, 0], ff_w13[layer, 0].astype(BF16), ff_w2[layer, 0].astype(BF16), 0, tm)
        if layer % 2 == 0:
            e = layer // 2
            lambda_init = 0.8 - 0.6 * math.exp(-0.3 * layer)
            gains = jnp.stack([a_qk_norm[e, 0], a_qk_norm[e, 1], b_qk_norm[e, 0], b_qk_norm[e, 1]])
            gains = jnp.broadcast_to(gains[:, :, None], (4, HEAD_DIM, tm))
            qa_t, ka, va_t, qb_t, kb, vb_t = _proj(x, m, norm_g[layer, 1], w_in[e].T.astype(BF16),
                                                   gains, cos_t, sin_t, tm)
            mix_a = _dattn(qa_t, ka, va_t, a_lambda[e], a_subln[e], lambda_init,
                           _tile(s, 512), _tile(s, 512))
            mix_b = _na(qb_t, kb, vb_t, _na_bias_table(b_rpb[e]))
            x = _outproj(x, mix_a, mix_b, m, w_out[e].astype(BF16), tm)
        else:
            o = layer // 2
            x = _pool(x, m, norm_g[layer, 1], pool_w[o].astype(BF16), pool_b[o], pool_scale[o], tm)
        x = _ffn(x, m, norm_g[layer, 2], ff_w13[layer, 1].astype(BF16), ff_w2[layer, 1].astype(BF16), 6, tm)
    return x
```

```python
import functools
import math

import numpy as np
import jax
import jax.numpy as jnp
from jax import lax
from jax.experimental import pallas as pl
from jax.experimental.pallas import tpu as pltpu

F32 = jnp.float32
BF16 = jnp.bfloat16

D_MODEL = 1024
HEAD_DIM = 64
A_HEADS = 4
A_VDIM = 2 * HEAD_DIM
B_HEADS = 8
D_FF = 2816
GRID_W = 64
NA_ROWS = 8
NA_COLS = 16
POOL_WINDOWS = (2, 4, 8, 16)
POOL_GROUP_DIM = D_MODEL // len(POOL_WINDOWS)
ROPE_THETA = 10000.0
EPS = 1e-6
N_MOD = 9
NEG_INF = -1e30
QK_SCALE = HEAD_DIM ** -0.5

V7X_VMEM_LIMIT_BYTES = 56 * 1024 * 1024
BF16_SUBLANES = 16
FF_CHUNK = 256
DA_QSUB = 256
DA_KBLK = 4096
DA_BOUND_MARGIN = 1.02
DA_MIN_SUM = 1e-18
DA_SAFE_KBLK = 256
DA_VROWS = A_VDIM + BF16_SUBLANES
NA_BLOCK_ROWS = 4
NA_WIN_ROWS = 3 * NA_BLOCK_ROWS
POOL_HALO = 8


def _params(*sem):
    return pltpu.CompilerParams(dimension_semantics=sem, vmem_limit_bytes=V7X_VMEM_LIMIT_BYTES)


def _resident(shape, index_map):
    return pl.BlockSpec(shape, index_map, pipeline_mode=pl.Buffered(1))


def _norm_mod(x, g, shift, scale):
    ms = jnp.mean(x * x, axis=-1, keepdims=True)
    y = (x * lax.rsqrt(ms + EPS)) * g
    return y * (1.0 + scale) + shift


def _adaln_kernel(c_ref, w_ref, b_ref, o_ref):
    c = c_ref[...]
    cond = c / (1.0 + jnp.exp(-c))
    o_ref[...] = jnp.dot(cond, w_ref[...], preferred_element_type=F32,
                         precision=lax.Precision.HIGHEST) + b_ref[...]


def _adaln(c, ada_w, ada_b):
    depth, d, n = ada_w.shape
    b = c.shape[0]
    bp = -(-b // 8) * 8
    cp = jnp.pad(c, ((0, bp - b), (0, 0)))
    tn = 1024
    out = pl.pallas_call(
        _adaln_kernel,
        grid=(depth, n // tn),
        in_specs=[
            pl.BlockSpec((bp, d), lambda l, j: (0, 0)),
            pl.BlockSpec((None, d, tn), lambda l, j: (l, 0, j)),
            pl.BlockSpec((None, 1, tn), lambda l, j: (l, 0, j)),
        ],
        out_specs=pl.BlockSpec((None, bp, tn), lambda l, j: (l, 0, j)),
        out_shape=jax.ShapeDtypeStruct((depth, bp, n), F32),
        compiler_params=_params("arbitrary", "arbitrary"),
        name="adaln",
    )(cp, ada_w, ada_b.reshape(depth, 1, n))
    return out[:, :b].reshape(depth, b, N_MOD, d)


def _ffn_kernel(x_ref, mod_ref, g_ref, w13_ref, w2_ref, o_ref, act_ref, *, row0):
    x = x_ref[...]
    h = _norm_mod(x, g_ref[...], mod_ref[row0:row0 + 1, :], mod_ref[row0 + 1:row0 + 2, :]).astype(BF16)
    for ci in range(D_FF // FF_CHUNK):
        lo = ci * FF_CHUNK
        a = jnp.dot(h, w13_ref[:, lo:lo + FF_CHUNK], preferred_element_type=F32)
        b = jnp.dot(h, w13_ref[:, D_FF + lo:D_FF + lo + FF_CHUNK], preferred_element_type=F32)
        act_ref[:, lo:lo + FF_CHUNK] = ((a / (1.0 + jnp.exp(-a))) * b).astype(BF16)
    y = jnp.dot(act_ref[...], w2_ref[...], preferred_element_type=F32)
    o_ref[...] = x + (0.5 * mod_ref[row0 + 2:row0 + 3, :]) * y


def _ffn(x, mod, g, w13, w2, row0, tm):
    b, s, d = x.shape
    return pl.pallas_call(
        functools.partial(_ffn_kernel, row0=row0),
        grid=(b, s // tm),
        in_specs=[
            pl.BlockSpec((None, tm, d), lambda bi, i: (bi, i, 0)),
            pl.BlockSpec((None, N_MOD, d), lambda bi, i: (bi, 0, 0)),
            pl.BlockSpec((1, d), lambda bi, i: (0, 0)),
            _resident((d, 2 * D_FF), lambda bi, i: (0, 0)),
            _resident((D_FF, d), lambda bi, i: (0, 0)),
        ],
        out_specs=pl.BlockSpec((None, tm, d), lambda bi, i: (bi, i, 0)),
        out_shape=jax.ShapeDtypeStruct(x.shape, F32),
        scratch_shapes=[pltpu.VMEM((tm, D_FF), BF16)],
        compiler_params=_params("parallel", "parallel"),
        name="ffn",
    )(x, mod, g.reshape(1, d), w13, w2)


def _proj_kernel(x_ref, mod_ref, g_ref, wt_ref, gain_ref, cos_ref, sin_ref,
                 qa_ref, ka_ref, va_ref, qb_ref, kb_ref, vb_ref):
    tm = x_ref.shape[0]
    h = _norm_mod(x_ref[...], g_ref[...], mod_ref[3:4, :], mod_ref[4:5, :]).astype(BF16)
    cos = cos_ref[...]
    sin = sin_ref[...]
    width = B_HEADS * HEAD_DIM

    def group_t(gi):
        return lax.dot_general(wt_ref[gi * width:(gi + 1) * width, :], h,
                               (((1,), (1,)), ((), ())), preferred_element_type=F32)

    def head_norm(ut, gain):
        x3 = ut.reshape(B_HEADS, HEAD_DIM, tm)
        ms = jnp.mean(x3 * x3, axis=1, keepdims=True)
        return (x3 * lax.rsqrt(ms + EPS)) * gain[None]

    def rope(x3):
        x1 = x3[:, :HEAD_DIM // 2, :]
        x2 = x3[:, HEAD_DIM // 2:, :]
        return jnp.concatenate([x1 * cos - x2 * sin, x1 * sin + x2 * cos], axis=1)

    qa = rope(head_norm(group_t(0), gain_ref[0])) * QK_SCALE
    qa_ref[...] = qa.reshape(A_HEADS, A_VDIM, tm).astype(BF16)
    ka = rope(head_norm(group_t(1), gain_ref[1])).reshape(A_HEADS, A_VDIM, tm)
    for hh in range(A_HEADS):
        ka_ref[hh] = ka[hh].T.astype(BF16)
    va_ref[:, :A_VDIM, :] = group_t(2).reshape(A_HEADS, A_VDIM, tm).astype(BF16)
    va_ref[:, A_VDIM:, :] = jnp.ones((A_HEADS, DA_VROWS - A_VDIM, tm), BF16)
    qb = head_norm(group_t(3), gain_ref[2]) * QK_SCALE
    qb_ref[...] = qb.astype(BF16)
    kb = head_norm(group_t(4), gain_ref[3]).reshape(B_HEADS // 2, 2 * HEAD_DIM, tm)
    for hp in range(B_HEADS // 2):
        kb_ref[hp] = kb[hp].T.astype(BF16)
    vb_ref[...] = group_t(5).reshape(B_HEADS, HEAD_DIM, tm).astype(BF16)


def _proj(x, mod, g, w_in_t, gains, cos_t, sin_t, tm):
    b, s, d = x.shape
    tok = lambda bi, i: (bi, 0, 0, i)
    seq = lambda bi, i: (bi, 0, i, 0)
    return pl.pallas_call(
        _proj_kernel,
        grid=(b, s // tm),
        in_specs=[
            pl.BlockSpec((None, tm, d), lambda bi, i: (bi, i, 0)),
            pl.BlockSpec((None, N_MOD, d), lambda bi, i: (bi, 0, 0)),
            pl.BlockSpec((1, d), lambda bi, i: (0, 0)),
            _resident(w_in_t.shape, lambda bi, i: (0, 0)),
            pl.BlockSpec((4, HEAD_DIM, tm), lambda bi, i: (0, 0, 0)),
            pl.BlockSpec((HEAD_DIM // 2, tm), lambda bi, i: (0, i)),
            pl.BlockSpec((HEAD_DIM // 2, tm), lambda bi, i: (0, i)),
        ],
        out_specs=[
            pl.BlockSpec((None, A_HEADS, A_VDIM, tm), tok),
            pl.BlockSpec((None, A_HEADS, tm, A_VDIM), seq),
            pl.BlockSpec((None, A_HEADS, DA_VROWS, tm), tok),
            pl.BlockSpec((None, B_HEADS, HEAD_DIM, tm), tok),
            pl.BlockSpec((None, B_HEADS // 2, tm, 2 * HEAD_DIM), seq),
            pl.BlockSpec((None, B_HEADS, HEAD_DIM, tm), tok),
        ],
        out_shape=[
            jax.ShapeDtypeStruct((b, A_HEADS, A_VDIM, s), BF16),
            jax.ShapeDtypeStruct((b, A_HEADS, s, A_VDIM), BF16),
            jax.ShapeDtypeStruct((b, A_HEADS, DA_VROWS, s), BF16),
            jax.ShapeDtypeStruct((b, B_HEADS, HEAD_DIM, s), BF16),
            jax.ShapeDtypeStruct((b, B_HEADS // 2, s, 2 * HEAD_DIM), BF16),
            jax.ShapeDtypeStruct((b, B_HEADS, HEAD_DIM, s), BF16),
        ],
        compiler_params=_params("parallel", "parallel"),
        name="qkv_proj",
    )(x, mod, g.reshape(1, d), w_in_t, gains, cos_t, sin_t)


def _dattn_kernel(q_ref, k_ref, v_ref, lam_ref, gn_ref, sg_ref, o_ref, acc_ref, *, kblk, lambda_init):
    s_len = k_ref.shape[0]
    qblk = q_ref.shape[1]
    lp = lam_ref[...]
    lam = (jnp.exp(jnp.sum(lp[0:1] * lp[1:2], axis=-1, keepdims=True))
           - jnp.exp(jnp.sum(lp[2:3] * lp[3:4], axis=-1, keepdims=True)) + lambda_init)
    gmax = jnp.max(jnp.abs(gn_ref[...]), axis=-1, keepdims=True)
    shift = (DA_BOUND_MARGIN * HEAD_DIM * QK_SCALE) * gmax[0:1] * gmax[1:2]
    z = jnp.zeros((HEAD_DIM, DA_QSUB), BF16)
    for sb in range(qblk // DA_QSUB):
        qt = q_ref[:, sb * DA_QSUB:(sb + 1) * DA_QSUB]
        rhs = jnp.concatenate([jnp.concatenate([qt[:HEAD_DIM], z], axis=0),
                               jnp.concatenate([z, qt[HEAD_DIM:]], axis=0)], axis=1)
        acc_ref[...] = jnp.zeros_like(acc_ref)

        def body(t, carry):
            k0 = pl.multiple_of(t * kblk, kblk)
            st = jnp.dot(k_ref[pl.ds(k0, kblk), :], rhs, preferred_element_type=F32)
            acc_ref[...] += jnp.dot(v_ref[:, pl.ds(k0, kblk)], jnp.exp(st - shift).astype(BF16),
                                    preferred_element_type=F32)
            return carry

        lax.fori_loop(0, s_len // kblk, body, 0)

        sums_ok = jnp.min(acc_ref[A_VDIM:A_VDIM + 1, :]) >= DA_MIN_SUM

        @pl.when(jnp.logical_not(sums_ok))
        def _():
            acc_ref[...] = jnp.zeros_like(acc_ref)

            def safe_body(t, m):
                k0 = pl.multiple_of(t * DA_SAFE_KBLK, DA_SAFE_KBLK)
                st = jnp.dot(k_ref[pl.ds(k0, DA_SAFE_KBLK), :], rhs, preferred_element_type=F32)
                m_new = jnp.maximum(m, jnp.max(st, axis=0, keepdims=True))
                pv = jnp.dot(v_ref[:, pl.ds(k0, DA_SAFE_KBLK)], jnp.exp(st - m_new).astype(BF16),
                             preferred_element_type=F32)
                acc_ref[...] = acc_ref[...] * jnp.exp(m - m_new) + pv
                return m_new

            lax.fori_loop(0, s_len // DA_SAFE_KBLK, safe_body, jnp.full((1, 2 * DA_QSUB), NEG_INF, F32))

        o = acc_ref[:A_VDIM, :] / acc_ref[A_VDIM:A_VDIM + 1, :]
        ot = o[:, :DA_QSUB] - lam * o[:, DA_QSUB:]
        ms = jnp.mean(ot * ot, axis=0, keepdims=True)
        y = ((ot * lax.rsqrt(ms + EPS)) * sg_ref[...]) * (1.0 - lambda_init)
        o_ref[sb * DA_QSUB:(sb + 1) * DA_QSUB, :] = y.T.astype(BF16)


def _dattn(qa_t, ka, va_t, a_lambda, a_qk_gain, a_subln, lambda_init, qblk):
    b, nh, _, s = qa_t.shape
    assert qblk % DA_QSUB == 0
    return pl.pallas_call(
        functools.partial(_dattn_kernel, kblk=_tile(s, DA_KBLK), lambda_init=lambda_init),
        grid=(b, nh, s // qblk),
        in_specs=[
            pl.BlockSpec((None, None, A_VDIM, qblk), lambda bi, h, i: (bi, h, 0, i)),
            pl.BlockSpec((None, None, s, A_VDIM), lambda bi, h, i: (bi, h, 0, 0)),
            pl.BlockSpec((None, None, DA_VROWS, s), lambda bi, h, i: (bi, h, 0, 0)),
            pl.BlockSpec((4, HEAD_DIM), lambda bi, h, i: (0, 0)),
            pl.BlockSpec((2, HEAD_DIM), lambda bi, h, i: (0, 0)),
            pl.BlockSpec((A_VDIM, DA_QSUB), lambda bi, h, i: (0, 0)),
        ],
        out_specs=pl.BlockSpec((None, qblk, A_VDIM), lambda bi, h, i: (bi, i, h)),
        out_shape=jax.ShapeDtypeStruct((b, s, nh * A_VDIM), BF16),
        scratch_shapes=[pltpu.VMEM((DA_VROWS, 2 * DA_QSUB), F32)],
        compiler_params=_params("parallel", "parallel", "arbitrary"),
        name="diff_attn",
    )(qa_t, ka, va_t, a_lambda, a_qk_gain, jnp.broadcast_to(a_subln[:, None], (A_VDIM, DA_QSUB)))


def _toeplitz_selector():
    kc = np.arange(GRID_W)[:, None]
    qc = np.arange(GRID_W)[None, :]
    cs = np.clip(qc - NA_COLS // 2, 0, GRID_W - NA_COLS)
    col_ok = (kc >= cs) & (kc < cs + NA_COLS)
    dc = np.clip(kc - qc + NA_COLS - 1, 0, 2 * NA_COLS - 2)
    sel = (np.arange(2 * NA_COLS)[:, None, None] == dc[None]) & col_ok[None]
    mask = np.where(col_ok, 0.0, NEG_INF)
    return (sel.reshape(2 * NA_COLS, GRID_W * GRID_W).astype(np.float32),
            mask.reshape(1, GRID_W * GRID_W).astype(np.float32))


def _rpb_expand_kernel(r_ref, sel_ref, mask_ref, o_ref):
    o_ref[...] = jnp.dot(r_ref[...], sel_ref[...], preferred_element_type=F32,
                         precision=lax.Precision.HIGHEST) + mask_ref[...]


def _na_bias_table(rpb):
    nh, ndr, ndc = rpb.shape
    rows = nh * ndr
    rows_p = -(-rows // 8) * 8
    sel, mask = _toeplitz_selector()
    r = jnp.pad(rpb.reshape(rows, ndc), ((0, rows_p - rows), (0, sel.shape[0] - ndc)))
    flat = pl.pallas_call(
        _rpb_expand_kernel,
        out_shape=jax.ShapeDtypeStruct((rows_p, GRID_W * GRID_W), F32),
        name="rpb_expand",
    )(r, sel, mask)
    tile = flat[:rows].reshape(nh, ndr, GRID_W, GRID_W)
    neg = jnp.full((nh, GRID_W, GRID_W), NEG_INF, F32)
    half = NA_ROWS // 2
    kinds = []
    for kind in range(3):
        win_rows = []
        for i in range(NA_WIN_ROWS):
            blocks = []
            for j in range(NA_BLOCK_ROWS):
                if kind == 0:
                    dr, ok = i - j + NA_ROWS - 1, i < NA_ROWS
                elif kind == 1:
                    dr, ok = i - j + half - 1, j <= i < j + NA_ROWS
                else:
                    dr, ok = i - j - 1, i >= NA_WIN_ROWS - NA_ROWS
                blocks.append(tile[:, dr] if ok else neg)
            win_rows.append(jnp.concatenate(blocks, axis=-1))
        kinds.append(jnp.concatenate(win_rows, axis=1))
    return jnp.stack(kinds)


def _na_kernel(q_ref, k_ref, v_ref, bias_ref, o_ref, *, nblk):
    qn = NA_BLOCK_ROWS * GRID_W
    win = NA_WIN_ROWS * GRID_W
    rb = pl.program_id(2)
    t0 = pl.multiple_of(jnp.clip(rb - 1, 0, nblk - 3) * qn, qn)
    kwin = k_ref[pl.ds(t0, win), :]
    z = jnp.zeros((HEAD_DIM, qn), BF16)
    outs = []
    for hh in range(2):
        qt = q_ref[hh]
        rhs = jnp.concatenate([qt, z] if hh == 0 else [z, qt], axis=0)
        st = jnp.dot(kwin, rhs, preferred_element_type=F32) + bias_ref[hh]
        m = jnp.max(st, axis=0, keepdims=True)
        p = jnp.exp(st - m)
        l = jnp.sum(p, axis=0, keepdims=True)
        pv = jnp.dot(v_ref[hh, :, pl.ds(t0, win)], p.astype(BF16), preferred_element_type=F32)
        outs.append(pv / l)
    o_ref[...] = jnp.concatenate(outs, axis=0).T.astype(BF16)


def _na(qb_t, kb, vb_t, bias_tab):
    b, nh, _, s = qb_t.shape
    qn = NA_BLOCK_ROWS * GRID_W
    win = NA_WIN_ROWS * GRID_W
    nblk = s // qn
    assert nblk >= 3

    def bias_idx(bi, hp, rb):
        kind = jnp.where(rb == 0, 0, jnp.where(rb == nblk - 1, 2, 1))
        return (kind, hp, 0, 0)

    return pl.pallas_call(
        functools.partial(_na_kernel, nblk=nblk),
        grid=(b, nh // 2, nblk),
        in_specs=[
            pl.BlockSpec((None, 2, HEAD_DIM, qn), lambda bi, hp, rb: (bi, hp, 0, rb)),
            pl.BlockSpec((None, None, s, 2 * HEAD_DIM), lambda bi, hp, rb: (bi, hp, 0, 0)),
            pl.BlockSpec((None, 2, HEAD_DIM, s), lambda bi, hp, rb: (bi, hp, 0, 0)),
            pl.BlockSpec((None, 2, win, qn), bias_idx),
        ],
        out_specs=pl.BlockSpec((None, qn, 2 * HEAD_DIM), lambda bi, hp, rb: (bi, rb, hp)),
        out_shape=jax.ShapeDtypeStruct((b, s, nh * HEAD_DIM), BF16),
        compiler_params=_params("parallel", "parallel", "arbitrary"),
        name="nbr_attn",
    )(qb_t, kb, vb_t, bias_tab)


def _outproj_kernel(x_ref, ma_ref, mb_ref, mod_ref, w_ref, o_ref):
    half = ma_ref.shape[1]
    y = (jnp.dot(ma_ref[...], w_ref[:half, :], preferred_element_type=F32)
         + jnp.dot(mb_ref[...], w_ref[half:, :], preferred_element_type=F32))
    o_ref[...] = x_ref[...] + mod_ref[5:6, :] * y


def _outproj(x, mix_a, mix_b, mod, w_out, tm):
    b, s, d = x.shape
    half = mix_a.shape[-1]
    return pl.pallas_call(
        _outproj_kernel,
        grid=(b, s // tm),
        in_specs=[
            pl.BlockSpec((None, tm, d), lambda bi, i: (bi, i, 0)),
            pl.BlockSpec((None, tm, half), lambda bi, i: (bi, i, 0)),
            pl.BlockSpec((None, tm, half), lambda bi, i: (bi, i, 0)),
            pl.BlockSpec((None, N_MOD, d), lambda bi, i: (bi, 0, 0)),
            _resident(w_out.shape, lambda bi, i: (0, 0)),
        ],
        out_specs=pl.BlockSpec((None, tm, d), lambda bi, i: (bi, i, 0)),
        out_shape=jax.ShapeDtypeStruct(x.shape, F32),
        compiler_params=_params("parallel", "parallel"),
        name="mix_out_proj",
    )(x, mix_a, mix_b, mod, w_out)


def _pool_kernel(x_ref, xp_ref, xn_ref, mod_ref, g_ref, w_ref, b_ref, sc_ref, o_ref, *, s_len):
    tm = x_ref.shape[0]
    i = pl.program_id(1)
    g = g_ref[...]
    shift = mod_ref[3:4, :]
    scale = mod_ref[4:5, :]
    x = x_ref[...]
    h = _norm_mod(x, g, shift, scale)
    hp = _norm_mod(xp_ref[...], g, shift, scale) * (i > 0).astype(F32)
    hn = _norm_mod(xn_ref[...], g, shift, scale) * (i < pl.num_programs(1) - 1).astype(F32)
    hext = jnp.concatenate([hp, h, hn], axis=0)
    t = i * tm + lax.broadcasted_iota(jnp.int32, (tm, 1), 0)
    outs = []
    for gi, w in enumerate(POOL_WINDOWS):
        lo = gi * POOL_GROUP_DIM
        run = hext[:, lo:lo + POOL_GROUP_DIM]
        span = 1
        while span < w:
            n = run.shape[0]
            run = run[:n - span] + run[span:]
            span *= 2
        start = POOL_HALO - w // 2
        seg = run[start:start + tm]
        cnt = (jnp.minimum(t + w // 2, s_len) - jnp.maximum(t - w // 2, 0)).astype(F32)
        y = seg / cnt - h[:, lo:lo + POOL_GROUP_DIM]
        outs.append(jnp.dot(y.astype(BF16), w_ref[gi], preferred_element_type=F32))
    y = (jnp.concatenate(outs, axis=-1) + b_ref[...]) * sc_ref[...]
    o_ref[...] = x + mod_ref[5:6, :] * y


def _pool(x, mod, g, pool_w, pool_b, pool_scale, tm):
    b, s, d = x.shape
    per = tm // POOL_HALO
    last = s // POOL_HALO - 1
    return pl.pallas_call(
        functools.partial(_pool_kernel, s_len=s),
        grid=(b, s // tm),
        in_specs=[
            pl.BlockSpec((None, tm, d), lambda bi, i: (bi, i, 0)),
            pl.BlockSpec((None, POOL_HALO, d), lambda bi, i: (bi, jnp.maximum(i * per - 1, 0), 0)),
            pl.BlockSpec((None, POOL_HALO, d), lambda bi, i: (bi, jnp.minimum((i + 1) * per, last), 0)),
            pl.BlockSpec((None, N_MOD, d), lambda bi, i: (bi, 0, 0)),
            pl.BlockSpec((1, d), lambda bi, i: (0, 0)),
            pl.BlockSpec(pool_w.shape, lambda bi, i: (0, 0, 0)),
            pl.BlockSpec((1, d), lambda bi, i: (0, 0)),
            pl.BlockSpec((1, d), lambda bi, i: (0, 0)),
        ],
        out_specs=pl.BlockSpec((None, tm, d), lambda bi, i: (bi, i, 0)),
        out_shape=jax.ShapeDtypeStruct(x.shape, F32),
        compiler_params=_params("parallel", "parallel"),
        name="pool_mixer",
    )(x, x, x, mod, g.reshape(1, d), pool_w, pool_b.reshape(1, d), pool_scale.reshape(1, d))


def _tile(s, want):
    return want if s % want == 0 else s


def kernel(x, c, ada_w, ada_b, norm_g, ff_w13, ff_w2, w_in, w_out, a_qk_norm, a_lambda, a_subln,
           b_qk_norm, b_rpb, pool_w, pool_b, pool_scale):
    b, s, d = x.shape
    depth = ada_w.shape[0]
    tm = _tile(s, 512)
    mod = _adaln(c, ada_w, ada_b)

    pos = jnp.arange(s, dtype=F32)
    inv = ROPE_THETA ** (-jnp.arange(0, HEAD_DIM, 2, dtype=F32) / HEAD_DIM)
    ang = inv[:, None] * pos[None, :]
    cos_t, sin_t = jnp.cos(ang), jnp.sin(ang)

    for layer in range(depth):
        m = mod[layer]
        x = _ffn(x, m, norm_g[layer, 0], ff_w13[layer, 0].astype(BF16), ff_w2[layer, 0].astype(BF16), 0, tm)
        if layer % 2 == 0:
            e = layer // 2
            lambda_init = 0.8 - 0.6 * math.exp(-0.3 * layer)
            gains = jnp.stack([a_qk_norm[e, 0], a_qk_norm[e, 1], b_qk_norm[e, 0], b_qk_norm[e, 1]])
            gains = jnp.broadcast_to(gains[:, :, None], (4, HEAD_DIM, tm))
            qa_t, ka, va_t, qb_t, kb, vb_t = _proj(x, m, norm_g[layer, 1], w_in[e].T.astype(BF16),
                                                   gains, cos_t, sin_t, tm)
            mix_a = _dattn(qa_t, ka, va_t, a_lambda[e], a_qk_norm[e], a_subln[e], lambda_init, _tile(s, 1024))
            mix_b = _na(qb_t, kb, vb_t, _na_bias_table(b_rpb[e]))
            x = _outproj(x, mix_a, mix_b, m, w_out[e].astype(BF16), tm)
        else:
            o = layer // 2
            x = _pool(x, m, norm_g[layer, 1], pool_w[o].astype(BF16), pool_b[o], pool_scale[o], tm)
        x = _ffn(x, m, norm_g[layer, 2], ff_w13[layer, 1].astype(BF16), ff_w2[layer, 1].astype(BF16), 6, tm)
    return x
```

```python
import functools
import math

import numpy as np
import jax
import jax.numpy as jnp
from jax import lax
from jax.experimental import pallas as pl
from jax.experimental.pallas import tpu as pltpu

F32 = jnp.float32
BF16 = jnp.bfloat16

D_MODEL = 1024
HEAD_DIM = 64
A_HEADS = 4
A_VDIM = 2 * HEAD_DIM
B_HEADS = 8
D_FF = 2816
GRID_W = 64
NA_ROWS = 8
NA_COLS = 16
POOL_WINDOWS = (2, 4, 8, 16)
POOL_GROUP_DIM = D_MODEL // len(POOL_WINDOWS)
ROPE_THETA = 10000.0
EPS = 1e-6
N_MOD = 9
NEG_INF = -1e30
QK_SCALE = HEAD_DIM ** -0.5

V7X_VMEM_LIMIT_BYTES = 56 * 1024 * 1024
BF16_SUBLANES = 16
TOKEN_TILE = 512
FFN_TOKEN_TILE = 1024
FF_CHUNK = 256
DA_QSUB = 256
DA_KBLK = 4096
DA_BOUND_MARGIN = 1.02
DA_MIN_SUM = 1e-18
DA_SAFE_KBLK = 256
DA_VROWS = A_VDIM + BF16_SUBLANES
NA_BLOCK_ROWS = 4
NA_WIN_ROWS = 3 * NA_BLOCK_ROWS
NA_VROWS = HEAD_DIM + BF16_SUBLANES
NA_DR_PAD = 16
NA_MIN_SUM = 1e-18
POOL_HALO = 8


def _params(*sem):
    return pltpu.CompilerParams(dimension_semantics=sem, vmem_limit_bytes=V7X_VMEM_LIMIT_BYTES)


def _resident(shape, index_map):
    return pl.BlockSpec(shape, index_map, pipeline_mode=pl.Buffered(1))


def _norm_mod(x, g, shift, scale):
    ms = jnp.mean(x * x, axis=-1, keepdims=True)
    y = (x * lax.rsqrt(ms + EPS)) * g
    return y * (1.0 + scale) + shift


def _adaln_kernel(c_ref, w_ref, b_ref, o_ref):
    c = c_ref[...]
    cond = c / (1.0 + jnp.exp(-c))
    o_ref[...] = jnp.dot(cond, w_ref[...], preferred_element_type=F32,
                         precision=lax.Precision.HIGHEST) + b_ref[...]


def _adaln(c, ada_w, ada_b):
    depth, d, n = ada_w.shape
    b = c.shape[0]
    bp = -(-b // 8) * 8
    cp = jnp.pad(c, ((0, bp - b), (0, 0)))
    tn = 1024
    out = pl.pallas_call(
        _adaln_kernel,
        grid=(depth, n // tn),
        in_specs=[
            pl.BlockSpec((bp, d), lambda l, j: (0, 0)),
            pl.BlockSpec((None, d, tn), lambda l, j: (l, 0, j)),
            pl.BlockSpec((None, 1, tn), lambda l, j: (l, 0, j)),
        ],
        out_specs=pl.BlockSpec((None, bp, tn), lambda l, j: (l, 0, j)),
        out_shape=jax.ShapeDtypeStruct((depth, bp, n), F32),
        compiler_params=_params("arbitrary", "arbitrary"),
        name="adaln",
    )(cp, ada_w, ada_b.reshape(depth, 1, n))
    return out[:, :b].reshape(depth, b, N_MOD, d)


def _swiglu_residual(x, mod_ref, g_ref, w13_ref, w2_ref, o_ref, act_ref, row0):
    h = _norm_mod(x, g_ref[...], mod_ref[row0:row0 + 1, :], mod_ref[row0 + 1:row0 + 2, :]).astype(BF16)
    for ci in range(D_FF // FF_CHUNK):
        lo = ci * FF_CHUNK
        a = jnp.dot(h, w13_ref[:, lo:lo + FF_CHUNK], preferred_element_type=F32)
        b = jnp.dot(h, w13_ref[:, D_FF + lo:D_FF + lo + FF_CHUNK], preferred_element_type=F32)
        act_ref[:, lo:lo + FF_CHUNK] = ((a / (1.0 + jnp.exp(-a))) * b).astype(BF16)
    y = jnp.dot(act_ref[...], w2_ref[...], preferred_element_type=F32)
    o_ref[...] = x + (0.5 * mod_ref[row0 + 2:row0 + 3, :]) * y


def _ffn_kernel(x_ref, mod_ref, g_ref, w13_ref, w2_ref, o_ref, act_ref):
    _swiglu_residual(x_ref[...], mod_ref, g_ref, w13_ref, w2_ref, o_ref, act_ref, 0)


def _attn_mix_ffn_kernel(x_ref, ma_ref, mb_ref, wo_ref, mod_ref, g_ref, w13_ref, w2_ref, o_ref, act_ref):
    half = ma_ref.shape[1]
    y = (jnp.dot(ma_ref[...], wo_ref[:half, :], preferred_element_type=F32)
         + jnp.dot(mb_ref[...], wo_ref[half:, :], preferred_element_type=F32))
    x = x_ref[...] + mod_ref[5:6, :] * y
    _swiglu_residual(x, mod_ref, g_ref, w13_ref, w2_ref, o_ref, act_ref, 6)


def _pool_mix_ffn_kernel(x_ref, xp_ref, xn_ref, gm_ref, pw_ref, pb_ref, ps_ref, mod_ref, g_ref, w13_ref, w2_ref,
                         o_ref, act_ref, *, s_len):
    tm = x_ref.shape[0]
    i = pl.program_id(1)
    gm = gm_ref[...]
    shift = mod_ref[3:4, :]
    scale = mod_ref[4:5, :]
    x = x_ref[...]
    h = _norm_mod(x, gm, shift, scale)
    hp = _norm_mod(xp_ref[...], gm, shift, scale) * (i > 0).astype(F32)
    hn = _norm_mod(xn_ref[...], gm, shift, scale) * (i < pl.num_programs(1) - 1).astype(F32)
    hext = jnp.concatenate([hp, h, hn], axis=0)
    t_head = i * tm + lax.broadcasted_iota(jnp.int32, (POOL_HALO, 1), 0)
    t_tail = t_head + (tm - POOL_HALO)

    def count(t, w):
        return (jnp.minimum(t + w // 2, s_len) - jnp.maximum(t - w // 2, 0)).astype(F32)

    outs = []
    for gi, w in enumerate(POOL_WINDOWS):
        lo = gi * POOL_GROUP_DIM
        run = hext[:, lo:lo + POOL_GROUP_DIM]
        span = 1
        while span < w:
            n = run.shape[0]
            run = run[:n - span] + run[span:]
            span *= 2
        start = POOL_HALO - w // 2
        seg = run[start:start + tm]
        pooled = jnp.concatenate([seg[:POOL_HALO] / count(t_head, w),
                                  seg[POOL_HALO:tm - POOL_HALO] * (1.0 / w),
                                  seg[tm - POOL_HALO:] / count(t_tail, w)], axis=0)
        y = pooled - h[:, lo:lo + POOL_GROUP_DIM]
        outs.append(jnp.dot(y.astype(BF16), pw_ref[gi], preferred_element_type=F32))
    y = (jnp.concatenate(outs, axis=-1) + pb_ref[...]) * ps_ref[...]
    _swiglu_residual(x + mod_ref[5:6, :] * y, mod_ref, g_ref, w13_ref, w2_ref, o_ref, act_ref, 6)


def _ffn_call(body, name, x, mixer_inputs, mixer_specs, mod, g, w13, w2, tm):
    b, s, d = x.shape
    row = pl.BlockSpec((1, d), lambda bi, i: (0, 0))
    return pl.pallas_call(
        body,
        grid=(b, s // tm),
        in_specs=[pl.BlockSpec((None, tm, d), lambda bi, i: (bi, i, 0))] + list(mixer_specs) + [
            pl.BlockSpec((None, N_MOD, d), lambda bi, i: (bi, 0, 0)),
            row,
            _resident((d, 2 * D_FF), lambda bi, i: (0, 0)),
            _resident((D_FF, d), lambda bi, i: (0, 0)),
        ],
        out_specs=pl.BlockSpec((None, tm, d), lambda bi, i: (bi, i, 0)),
        out_shape=jax.ShapeDtypeStruct(x.shape, F32),
        scratch_shapes=[pltpu.VMEM((tm, D_FF), BF16)],
        compiler_params=_params("parallel", "parallel"),
        name=name,
    )(x, *mixer_inputs, mod, g.reshape(1, d), w13, w2)


def _ffn(x, mod, g, w13, w2, tm):
    return _ffn_call(_ffn_kernel, "ffn", x, (), (), mod, g, w13, w2, tm)


def _attn_mix_ffn(x, mix_a, mix_b, w_out, mod, g, w13, w2, tm):
    half = mix_a.shape[-1]
    mix = pl.BlockSpec((None, tm, half), lambda bi, i: (bi, i, 0))
    specs = [mix, mix, _resident(w_out.shape, lambda bi, i: (0, 0))]
    return _ffn_call(_attn_mix_ffn_kernel, "attn_mix_ffn", x, (mix_a, mix_b, w_out), specs, mod, g, w13, w2, tm)


def _pool_mix_ffn(x, g_mix, pool_w, pool_b, pool_scale, mod, g, w13, w2, tm):
    b, s, d = x.shape
    per = tm // POOL_HALO
    last = s // POOL_HALO - 1
    row = pl.BlockSpec((1, d), lambda bi, i: (0, 0))
    specs = [
        pl.BlockSpec((None, POOL_HALO, d), lambda bi, i: (bi, jnp.maximum(i * per - 1, 0), 0)),
        pl.BlockSpec((None, POOL_HALO, d), lambda bi, i: (bi, jnp.minimum((i + 1) * per, last), 0)),
        row,
        pl.BlockSpec(pool_w.shape, lambda bi, i: (0, 0, 0)),
        row,
        row,
    ]
    inputs = (x, x, g_mix.reshape(1, d), pool_w, pool_b.reshape(1, d), pool_scale.reshape(1, d))
    return _ffn_call(functools.partial(_pool_mix_ffn_kernel, s_len=s), "pool_mix_ffn", x, inputs, specs,
                     mod, g, w13, w2, tm)


def _proj_kernel(x_ref, mod_ref, g_ref, wt_ref, gain_ref, cos_ref, sin_ref,
                 qa_ref, ka_ref, va_ref, qb_ref, kb_ref, vb_ref):
    tm = x_ref.shape[0]
    h = _norm_mod(x_ref[...], g_ref[...], mod_ref[3:4, :], mod_ref[4:5, :]).astype(BF16)
    cos = cos_ref[...]
    sin = sin_ref[...]
    width = B_HEADS * HEAD_DIM

    def group_t(gi):
        return lax.dot_general(wt_ref[gi * width:(gi + 1) * width, :], h,
                               (((1,), (1,)), ((), ())), preferred_element_type=F32)

    def head_norm(ut, gain):
        x3 = ut.reshape(B_HEADS, HEAD_DIM, tm)
        ms = jnp.mean(x3 * x3, axis=1, keepdims=True)
        return (x3 * lax.rsqrt(ms + EPS)) * gain[None]

    def rope(x3):
        x1 = x3[:, :HEAD_DIM // 2, :]
        x2 = x3[:, HEAD_DIM // 2:, :]
        return jnp.concatenate([x1 * cos - x2 * sin, x1 * sin + x2 * cos], axis=1)

    qa = rope(head_norm(group_t(0), gain_ref[0])) * QK_SCALE
    qa_ref[...] = qa.reshape(A_HEADS, A_VDIM, tm).astype(BF16)
    ka = rope(head_norm(group_t(1), gain_ref[1])).reshape(A_HEADS, A_VDIM, tm)
    for hh in range(A_HEADS):
        ka_ref[hh] = ka[hh].T.astype(BF16)
    qb = head_norm(group_t(3), gain_ref[2]) * QK_SCALE
    qb_ref[...] = qb.astype(BF16)
    kb = head_norm(group_t(4), gain_ref[3]).reshape(B_HEADS // 2, 2 * HEAD_DIM, tm)
    for hp in range(B_HEADS // 2):
        kb_ref[hp] = kb[hp].T.astype(BF16)
    va_ref[:, :A_VDIM, :] = group_t(2).reshape(A_HEADS, A_VDIM, tm).astype(BF16)
    va_ref[:, A_VDIM:, :] = jnp.ones((A_HEADS, DA_VROWS - A_VDIM, tm), BF16)
    vb_ref[:, :HEAD_DIM, :] = group_t(5).reshape(B_HEADS, HEAD_DIM, tm).astype(BF16)
    vb_ref[:, HEAD_DIM:, :] = jnp.ones((B_HEADS, NA_VROWS - HEAD_DIM, tm), BF16)


def _proj(x, mod, g, w_in_t, gains, cos_t, sin_t, tm):
    b, s, d = x.shape
    tok = lambda bi, i: (bi, 0, 0, i)
    seq = lambda bi, i: (bi, 0, i, 0)
    return pl.pallas_call(
        _proj_kernel,
        grid=(b, s // tm),
        in_specs=[
            pl.BlockSpec((None, tm, d), lambda bi, i: (bi, i, 0)),
            pl.BlockSpec((None, N_MOD, d), lambda bi, i: (bi, 0, 0)),
            pl.BlockSpec((1, d), lambda bi, i: (0, 0)),
            _resident(w_in_t.shape, lambda bi, i: (0, 0)),
            pl.BlockSpec((4, HEAD_DIM, tm), lambda bi, i: (0, 0, 0)),
            pl.BlockSpec((HEAD_DIM // 2, tm), lambda bi, i: (0, i)),
            pl.BlockSpec((HEAD_DIM // 2, tm), lambda bi, i: (0, i)),
        ],
        out_specs=[
            pl.BlockSpec((None, A_HEADS, A_VDIM, tm), tok),
            pl.BlockSpec((None, A_HEADS, tm, A_VDIM), seq),
            pl.BlockSpec((None, A_HEADS, DA_VROWS, tm), tok),
            pl.BlockSpec((None, B_HEADS, HEAD_DIM, tm), tok),
            pl.BlockSpec((None, B_HEADS // 2, tm, 2 * HEAD_DIM), seq),
            pl.BlockSpec((None, B_HEADS, NA_VROWS, tm), tok),
        ],
        out_shape=[
            jax.ShapeDtypeStruct((b, A_HEADS, A_VDIM, s), BF16),
            jax.ShapeDtypeStruct((b, A_HEADS, s, A_VDIM), BF16),
            jax.ShapeDtypeStruct((b, A_HEADS, DA_VROWS, s), BF16),
            jax.ShapeDtypeStruct((b, B_HEADS, HEAD_DIM, s), BF16),
            jax.ShapeDtypeStruct((b, B_HEADS // 2, s, 2 * HEAD_DIM), BF16),
            jax.ShapeDtypeStruct((b, B_HEADS, NA_VROWS, s), BF16),
        ],
        compiler_params=_params("parallel", "parallel"),
        name="qkv_proj",
    )(x, mod, g.reshape(1, d), w_in_t, gains, cos_t, sin_t)


def _dattn_kernel(q_ref, k_ref, v_ref, lam_ref, gn_ref, sg_ref, o_ref, acc_ref, *, kblk, lambda_init):
    s_len = k_ref.shape[0]
    n_strips = q_ref.shape[1] // DA_QSUB
    lp = lam_ref[...]
    lam = (jnp.exp(jnp.sum(lp[0:1] * lp[1:2], axis=-1, keepdims=True))
           - jnp.exp(jnp.sum(lp[2:3] * lp[3:4], axis=-1, keepdims=True)) + lambda_init)
    gmax = jnp.max(jnp.abs(gn_ref[...]), axis=-1, keepdims=True)
    shift = (DA_BOUND_MARGIN * HEAD_DIM * QK_SCALE) * gmax[0:1] * gmax[1:2]
    z = jnp.zeros((HEAD_DIM, DA_QSUB), BF16)

    def strip_rhs(sb):
        qt = q_ref[:, sb * DA_QSUB:(sb + 1) * DA_QSUB]
        return jnp.concatenate([jnp.concatenate([qt[:HEAD_DIM], z], axis=0),
                                jnp.concatenate([z, qt[HEAD_DIM:]], axis=0)], axis=1)

    def finish(sb, num, den):
        o = num / den
        ot = o[:, :DA_QSUB] - lam * o[:, DA_QSUB:]
        ms = jnp.mean(ot * ot, axis=0, keepdims=True)
        y = ((ot * lax.rsqrt(ms + EPS)) * sg_ref[...]) * (1.0 - lambda_init)
        o_ref[sb * DA_QSUB:(sb + 1) * DA_QSUB, :] = y.T.astype(BF16)

    min_sum = None
    for sb in range(n_strips):
        rhs = strip_rhs(sb)
        num = None
        den8 = None
        for k0 in range(0, s_len, kblk):
            st = jnp.dot(k_ref[k0:k0 + kblk, :], rhs, preferred_element_type=F32)
            p = jnp.exp(st - shift)
            part = jnp.sum(p.reshape(kblk // 8, 8, 2 * DA_QSUB), axis=0)
            pv = jnp.dot(v_ref[:A_VDIM, k0:k0 + kblk], p.astype(BF16), preferred_element_type=F32)
            num = pv if num is None else num + pv
            den8 = part if den8 is None else den8 + part
        den = jnp.sum(den8, axis=0, keepdims=True)
        finish(sb, num, den)
        min_sum = den if min_sum is None else jnp.minimum(min_sum, den)

    sums_ok = jnp.min(min_sum) >= DA_MIN_SUM

    @pl.when(jnp.logical_not(sums_ok))
    def _():
        for sb in range(n_strips):
            rhs = strip_rhs(sb)
            acc_ref[...] = jnp.zeros_like(acc_ref)

            def safe_body(t, m):
                k0 = pl.multiple_of(t * DA_SAFE_KBLK, DA_SAFE_KBLK)
                st = jnp.dot(k_ref[pl.ds(k0, DA_SAFE_KBLK), :], rhs, preferred_element_type=F32)
                m_new = jnp.maximum(m, jnp.max(st, axis=0, keepdims=True))
                pv = jnp.dot(v_ref[:, pl.ds(k0, DA_SAFE_KBLK)], jnp.exp(st - m_new).astype(BF16),
                             preferred_element_type=F32)
                acc_ref[...] = acc_ref[...] * jnp.exp(m - m_new) + pv
                return m_new

            lax.fori_loop(0, s_len // DA_SAFE_KBLK, safe_body, jnp.full((1, 2 * DA_QSUB), NEG_INF, F32))
            finish(sb, acc_ref[:A_VDIM, :], acc_ref[A_VDIM:A_VDIM + 1, :])


def _dattn(qa_t, ka, va_t, a_lambda, a_qk_gain, a_subln, lambda_init, qblk):
    b, nh, _, s = qa_t.shape
    assert qblk % DA_QSUB == 0
    return pl.pallas_call(
        functools.partial(_dattn_kernel, kblk=_tile(s, DA_KBLK), lambda_init=lambda_init),
        grid=(b, nh, s // qblk),
        in_specs=[
            pl.BlockSpec((None, None, A_VDIM, qblk), lambda bi, h, i: (bi, h, 0, i)),
            pl.BlockSpec((None, None, s, A_VDIM), lambda bi, h, i: (bi, h, 0, 0)),
            pl.BlockSpec((None, None, DA_VROWS, s), lambda bi, h, i: (bi, h, 0, 0)),
            pl.BlockSpec((4, HEAD_DIM), lambda bi, h, i: (0, 0)),
            pl.BlockSpec((2, HEAD_DIM), lambda bi, h, i: (0, 0)),
            pl.BlockSpec((A_VDIM, DA_QSUB), lambda bi, h, i: (0, 0)),
        ],
        out_specs=pl.BlockSpec((None, qblk, A_VDIM), lambda bi, h, i: (bi, i, h)),
        out_shape=jax.ShapeDtypeStruct((b, s, nh * A_VDIM), BF16),
        scratch_shapes=[pltpu.VMEM((DA_VROWS, 2 * DA_QSUB), F32)],
        compiler_params=_params("parallel", "parallel", "arbitrary"),
        name="diff_attn",
    )(qa_t, ka, va_t, a_lambda, a_qk_gain, jnp.broadcast_to(a_subln[:, None], (A_VDIM, DA_QSUB)))


def _toeplitz_selector():
    kc = np.arange(GRID_W)[:, None]
    qc = np.arange(GRID_W)[None, :]
    cs = np.clip(qc - NA_COLS // 2, 0, GRID_W - NA_COLS)
    col_ok = (kc >= cs) & (kc < cs + NA_COLS)
    dc = np.clip(kc - qc + NA_COLS - 1, 0, 2 * NA_COLS - 2)
    sel = (np.arange(2 * NA_COLS)[:, None, None] == dc[None]) & col_ok[None]
    mask = np.where(col_ok, 0.0, NEG_INF)
    return (sel.reshape(2 * NA_COLS, GRID_W * GRID_W).astype(np.float32),
            mask.reshape(1, GRID_W * GRID_W).astype(np.float32))


def _rpb_expand_kernel(r_ref, gn_ref, sel_ref, mask_ref, o_ref, *, ndr, ndc):
    r = r_ref[...]
    nh = r.shape[0] // NA_DR_PAD
    rows = lax.broadcasted_iota(jnp.int32, r.shape, 0)
    cols = lax.broadcasted_iota(jnp.int32, r.shape, 1)
    real = ((rows & (NA_DR_PAD - 1)) < ndr) & (cols < ndc)
    row_max = jnp.max(jnp.where(real, r, NEG_INF), axis=1, keepdims=True)
    head_max = jnp.max(row_max.reshape(nh, NA_DR_PAD, 1), axis=1, keepdims=True)
    gmax = jnp.max(jnp.abs(gn_ref[...]), axis=-1, keepdims=True)
    bound = (DA_BOUND_MARGIN * HEAD_DIM * QK_SCALE) * gmax[0:1] * gmax[1:2]
    shift = jnp.broadcast_to(head_max + bound[None], (nh, NA_DR_PAD, 1)).reshape(r.shape[0], 1)
    o_ref[...] = (jnp.dot(r, sel_ref[...], preferred_element_type=F32, precision=lax.Precision.HIGHEST)
                  + mask_ref[...]) - shift


def _na_bias_table(rpb, qk_gain):
    nh, ndr, ndc = rpb.shape
    sel, mask = _toeplitz_selector()
    r = jnp.pad(rpb, ((0, 0), (0, NA_DR_PAD - ndr), (0, sel.shape[0] - ndc))).reshape(nh * NA_DR_PAD, sel.shape[0])
    flat = pl.pallas_call(
        functools.partial(_rpb_expand_kernel, ndr=ndr, ndc=ndc),
        out_shape=jax.ShapeDtypeStruct((nh * NA_DR_PAD, GRID_W * GRID_W), F32),
        name="rpb_expand",
    )(r, qk_gain, sel, mask)
    tile = flat.reshape(nh, NA_DR_PAD, GRID_W, GRID_W)
    neg = jnp.full((nh, GRID_W, GRID_W), NEG_INF, F32)
    half = NA_ROWS // 2
    kinds = []
    for kind in range(3):
        win_rows = []
        for i in range(NA_WIN_ROWS):
            blocks = []
            for j in range(NA_BLOCK_ROWS):
                if kind == 0:
                    dr, ok = i - j + NA_ROWS - 1, i < NA_ROWS
                elif kind == 1:
                    dr, ok = i - j + half - 1, j <= i < j + NA_ROWS
                else:
                    dr, ok = i - j - 1, i >= NA_WIN_ROWS - NA_ROWS
                blocks.append(tile[:, dr] if ok else neg)
            win_rows.append(jnp.concatenate(blocks, axis=-1))
        kinds.append(jnp.concatenate(win_rows, axis=1))
    return jnp.stack(kinds)


def _na_kernel(q_ref, k_ref, v_ref, bias_ref, o_ref, *, nblk):
    qn = NA_BLOCK_ROWS * GRID_W
    win = NA_WIN_ROWS * GRID_W
    nh = q_ref.shape[0]
    rb = pl.program_id(1)
    t0 = pl.multiple_of(jnp.clip(rb - 1, 0, nblk - 3) * qn, qn)
    z = jnp.zeros((HEAD_DIM, qn), BF16)

    def logits(h):
        qt = q_ref[h]
        rhs = jnp.concatenate([qt, z] if h % 2 == 0 else [z, qt], axis=0)
        kwin = k_ref[h // 2, pl.ds(t0, win), :]
        return jnp.dot(kwin, rhs, preferred_element_type=F32) + bias_ref[h]

    def weighted(h, p):
        return jnp.dot(v_ref[h, :, pl.ds(t0, win)], p.astype(BF16), preferred_element_type=F32)

    def normalised(pvs):
        outs = [pv[:HEAD_DIM] / pv[HEAD_DIM:HEAD_DIM + 1] for pv in pvs]
        return jnp.concatenate(outs, axis=0).T.astype(BF16)

    sts = [logits(h) for h in range(nh)]
    pvs = [weighted(h, jnp.exp(st)) for h, st in enumerate(sts)]
    o_ref[...] = normalised(pvs)

    sums = functools.reduce(jnp.minimum, [pv[HEAD_DIM:HEAD_DIM + 1] for pv in pvs])
    sums_ok = jnp.min(sums) >= NA_MIN_SUM

    @pl.when(jnp.logical_not(sums_ok))
    def _():
        safe = []
        for h in range(nh):
            st = logits(h)
            safe.append(weighted(h, jnp.exp(st - jnp.max(st, axis=0, keepdims=True))))
        o_ref[...] = normalised(safe)


def _na(qb_t, kb, vb_t, bias_tab):
    b, nh, _, s = qb_t.shape
    qn = NA_BLOCK_ROWS * GRID_W
    win = NA_WIN_ROWS * GRID_W
    nblk = s // qn
    assert nblk >= 3

    def bias_idx(bi, rb):
        kind = jnp.where(rb == 0, 0, jnp.where(rb == nblk - 1, 2, 1))
        return (kind, 0, 0, 0)

    return pl.pallas_call(
        functools.partial(_na_kernel, nblk=nblk),
        grid=(b, nblk),
        in_specs=[
            pl.BlockSpec((None, nh, HEAD_DIM, qn), lambda bi, rb: (bi, 0, 0, rb)),
            _resident((None, nh // 2, s, 2 * HEAD_DIM), lambda bi, rb: (bi, 0, 0, 0)),
            _resident((None, nh, NA_VROWS, s), lambda bi, rb: (bi, 0, 0, 0)),
            pl.BlockSpec((None, nh, win, qn), bias_idx),
        ],
        out_specs=pl.BlockSpec((None, qn, nh * HEAD_DIM), lambda bi, rb: (bi, rb, 0)),
        out_shape=jax.ShapeDtypeStruct((b, s, nh * HEAD_DIM), BF16),
        compiler_params=_params("parallel", "arbitrary"),
        name="nbr_attn",
    )(qb_t, kb, vb_t, bias_tab)


def _tile(s, want):
    return want if s % want == 0 else s


def kernel(x, c, ada_w, ada_b, norm_g, ff_w13, ff_w2, w_in, w_out, a_qk_norm, a_lambda, a_subln,
           b_qk_norm, b_rpb, pool_w, pool_b, pool_scale):
    b, s, d = x.shape
    depth = ada_w.shape[0]
    tm = _tile(s, TOKEN_TILE)
    ffn_tm = _tile(s, FFN_TOKEN_TILE)
    mod = _adaln(c, ada_w, ada_b)

    pos = jnp.arange(s, dtype=F32)
    inv = ROPE_THETA ** (-jnp.arange(0, HEAD_DIM, 2, dtype=F32) / HEAD_DIM)
    ang = inv[:, None] * pos[None, :]
    cos_t, sin_t = jnp.cos(ang), jnp.sin(ang)

    for layer in range(depth):
        m = mod[layer]
        x = _ffn(x, m, norm_g[layer, 0], ff_w13[layer, 0].astype(BF16), ff_w2[layer, 0].astype(BF16), ffn_tm)
        tail = (m, norm_g[layer, 2], ff_w13[layer, 1].astype(BF16), ff_w2[layer, 1].astype(BF16), ffn_tm)
        if layer % 2 == 0:
            e = layer // 2
            lambda_init = 0.8 - 0.6 * math.exp(-0.3 * layer)
            gains = jnp.stack([a_qk_norm[e, 0], a_qk_norm[e, 1], b_qk_norm[e, 0], b_qk_norm[e, 1]])
            gains = jnp.broadcast_to(gains[:, :, None], (4, HEAD_DIM, tm))
            qa_t, ka, va_t, qb_t, kb, vb_t = _proj(x, m, norm_g[layer, 1], w_in[e].T.astype(BF16),
                                                   gains, cos_t, sin_t, tm)
            mix_a = _dattn(qa_t, ka, va_t, a_lambda[e], a_qk_norm[e], a_subln[e], lambda_init, _tile(s, 1024))
            mix_b = _na(qb_t, kb, vb_t, _na_bias_table(b_rpb[e], b_qk_norm[e]))
            x = _attn_mix_ffn(x, mix_a, mix_b, w_out[e].astype(BF16), *tail)
        else:
            o = layer // 2
            x = _pool_mix_ffn(x, norm_g[layer, 1], pool_w[o].astype(BF16), pool_b[o], pool_scale[o], *tail)
    return x
```

```python
import functools
import math

import numpy as np
import jax
import jax.numpy as jnp
from jax import lax
from jax.experimental import pallas as pl
from jax.experimental.pallas import tpu as pltpu

F32 = jnp.float32
BF16 = jnp.bfloat16

D_MODEL = 1024
HEAD_DIM = 64
A_HEADS = 4
A_VDIM = 2 * HEAD_DIM
B_HEADS = 8
D_FF = 2816
GRID_W = 64
NA_ROWS = 8
NA_COLS = 16
POOL_WINDOWS = (2, 4, 8, 16)
POOL_GROUP_DIM = D_MODEL // len(POOL_WINDOWS)
ROPE_THETA = 10000.0
EPS = 1e-6
N_MOD = 9
NEG_INF = -1e30
QK_SCALE = HEAD_DIM ** -0.5

V7X_VMEM_LIMIT_BYTES = 56 * 1024 * 1024
BF16_SUBLANES = 16
TOKEN_TILE = 512
FFN_TOKEN_TILE = 1024
FFN_ROW_BLOCK = 512
FF_CHUNK = 256
DA_QBLK = 1024
DA_QSUB = 256
DA_KBLK = 4096
DA_BOUND_MARGIN = 1.02
DA_MIN_SUM = 1e-18
DA_SAFE_KBLK = 256
DA_VROWS = A_VDIM + BF16_SUBLANES
NA_BLOCK_ROWS = 4
NA_WIN_ROWS = 3 * NA_BLOCK_ROWS
NA_VROWS = HEAD_DIM + BF16_SUBLANES
NA_DR_PAD = 16
NA_MIN_SUM = 1e-18
POOL_HALO = 8


def _params(*sem):
    return pltpu.CompilerParams(dimension_semantics=sem, vmem_limit_bytes=V7X_VMEM_LIMIT_BYTES)


def _resident(shape, index_map):
    return pl.BlockSpec(shape, index_map, pipeline_mode=pl.Buffered(1))


def _norm_mod(x, g, shift, scale):
    ms = jnp.mean(x * x, axis=-1, keepdims=True)
    y = (x * lax.rsqrt(ms + EPS)) * g
    return y * (1.0 + scale) + shift


def _adaln_kernel(c_ref, w_ref, b_ref, o_ref):
    c = c_ref[...]
    cond = c / (1.0 + jnp.exp(-c))
    o_ref[...] = jnp.dot(cond, w_ref[...], preferred_element_type=F32,
                         precision=lax.Precision.HIGHEST) + b_ref[...]


def _adaln(c, ada_w, ada_b):
    depth, d, n = ada_w.shape
    b = c.shape[0]
    bp = -(-b // 8) * 8
    cp = jnp.pad(c, ((0, bp - b), (0, 0)))
    tn = 1024
    out = pl.pallas_call(
        _adaln_kernel,
        grid=(depth, n // tn),
        in_specs=[
            pl.BlockSpec((bp, d), lambda l, j: (0, 0)),
            pl.BlockSpec((None, d, tn), lambda l, j: (l, 0, j)),
            pl.BlockSpec((None, 1, tn), lambda l, j: (l, 0, j)),
        ],
        out_specs=pl.BlockSpec((None, bp, tn), lambda l, j: (l, 0, j)),
        out_shape=jax.ShapeDtypeStruct((depth, bp, n), F32),
        compiler_params=_params("arbitrary", "arbitrary"),
        name="adaln",
    )(cp, ada_w, ada_b.reshape(depth, 1, n))
    return out[:, :b].reshape(depth, b, N_MOD, d)


def _swiglu_residual(x, mod_ref, g_ref, w13_ref, w2_ref, o_ref, act_ref, row0):
    tm = x.shape[0]
    rows = FFN_ROW_BLOCK if tm % FFN_ROW_BLOCK == 0 else tm
    for r0 in range(0, tm, rows):
        xr = x[r0:r0 + rows]
        h = _norm_mod(xr, g_ref[...], mod_ref[row0:row0 + 1, :], mod_ref[row0 + 1:row0 + 2, :]).astype(BF16)
        for ci in range(D_FF // FF_CHUNK):
            lo = ci * FF_CHUNK
            a = jnp.dot(h, w13_ref[:, lo:lo + FF_CHUNK], preferred_element_type=F32)
            b = jnp.dot(h, w13_ref[:, D_FF + lo:D_FF + lo + FF_CHUNK], preferred_element_type=F32)
            act_ref[r0:r0 + rows, lo:lo + FF_CHUNK] = ((a / (1.0 + jnp.exp(-a))) * b).astype(BF16)
        y = jnp.dot(act_ref[r0:r0 + rows, :], w2_ref[...], preferred_element_type=F32)
        o_ref[r0:r0 + rows, :] = xr + (0.5 * mod_ref[row0 + 2:row0 + 3, :]) * y


def _ffn_kernel(x_ref, mod_ref, g_ref, w13_ref, w2_ref, o_ref, act_ref):
    _swiglu_residual(x_ref[...], mod_ref, g_ref, w13_ref, w2_ref, o_ref, act_ref, 0)


def _attn_mix_ffn_kernel(x_ref, ma_ref, mb_ref, wo_ref, mod_ref, g_ref, w13_ref, w2_ref, o_ref, act_ref):
    half = ma_ref.shape[1]
    y = (jnp.dot(ma_ref[...], wo_ref[:half, :], preferred_element_type=F32)
         + jnp.dot(mb_ref[...], wo_ref[half:, :], preferred_element_type=F32))
    x = x_ref[...] + mod_ref[5:6, :] * y
    _swiglu_residual(x, mod_ref, g_ref, w13_ref, w2_ref, o_ref, act_ref, 6)


def _pool_mix_ffn_kernel(x_ref, xp_ref, xn_ref, gm_ref, pw_ref, pb_ref, ps_ref, mod_ref, g_ref, w13_ref, w2_ref,
                         o_ref, act_ref, *, s_len):
    tm = x_ref.shape[0]
    i = pl.program_id(1)
    gm = gm_ref[...]
    shift = mod_ref[3:4, :]
    scale = mod_ref[4:5, :]
    x = x_ref[...]
    h = _norm_mod(x, gm, shift, scale)
    hp = _norm_mod(xp_ref[...], gm, shift, scale) * (i > 0).astype(F32)
    hn = _norm_mod(xn_ref[...], gm, shift, scale) * (i < pl.num_programs(1) - 1).astype(F32)
    hext = jnp.concatenate([hp, h, hn], axis=0)
    t_head = i * tm + lax.broadcasted_iota(jnp.int32, (POOL_HALO, 1), 0)
    t_tail = t_head + (tm - POOL_HALO)

    def count(t, w):
        return (jnp.minimum(t + w // 2, s_len) - jnp.maximum(t - w // 2, 0)).astype(F32)

    outs = []
    for gi, w in enumerate(POOL_WINDOWS):
        lo = gi * POOL_GROUP_DIM
        run = hext[:, lo:lo + POOL_GROUP_DIM]
        span = 1
        while span < w:
            n = run.shape[0]
            run = run[:n - span] + run[span:]
            span *= 2
        start = POOL_HALO - w // 2
        seg = run[start:start + tm]
        pooled = jnp.concatenate([seg[:POOL_HALO] / count(t_head, w),
                                  seg[POOL_HALO:tm - POOL_HALO] * (1.0 / w),
                                  seg[tm - POOL_HALO:] / count(t_tail, w)], axis=0)
        y = pooled - h[:, lo:lo + POOL_GROUP_DIM]
        outs.append(jnp.dot(y.astype(BF16), pw_ref[gi], preferred_element_type=F32))
    y = (jnp.concatenate(outs, axis=-1) + pb_ref[...]) * ps_ref[...]
    _swiglu_residual(x + mod_ref[5:6, :] * y, mod_ref, g_ref, w13_ref, w2_ref, o_ref, act_ref, 6)


def _ffn_call(body, name, x, mixer_inputs, mixer_specs, mod, g, w13, w2, widx, tm):
    b, s, d = x.shape
    row = pl.BlockSpec((1, d), lambda bi, i: (0, 0))
    return pl.pallas_call(
        body,
        grid=(b, s // tm),
        in_specs=[pl.BlockSpec((None, tm, d), lambda bi, i: (bi, i, 0))] + list(mixer_specs) + [
            pl.BlockSpec((None, N_MOD, d), lambda bi, i: (bi, 0, 0)),
            row,
            _resident((None, None, d, 2 * D_FF), lambda bi, i: (*widx, 0, 0)),
            _resident((None, None, D_FF, d), lambda bi, i: (*widx, 0, 0)),
        ],
        out_specs=pl.BlockSpec((None, tm, d), lambda bi, i: (bi, i, 0)),
        out_shape=jax.ShapeDtypeStruct(x.shape, F32),
        scratch_shapes=[pltpu.VMEM((tm, D_FF), BF16)],
        compiler_params=_params("parallel", "parallel"),
        name=name,
    )(x, *mixer_inputs, mod, g.reshape(1, d), w13, w2)


def _ffn(x, mod, g, w13, w2, widx, tm):
    return _ffn_call(_ffn_kernel, "ffn", x, (), (), mod, g, w13, w2, widx, tm)


def _attn_mix_ffn(x, mix_a, mix_b, w_out, mod, g, w13, w2, widx, tm):
    half = mix_a.shape[-1]
    mix = pl.BlockSpec((None, tm, half), lambda bi, i: (bi, i, 0))
    specs = [mix, mix, _resident(w_out.shape, lambda bi, i: (0, 0))]
    return _ffn_call(_attn_mix_ffn_kernel, "attn_mix_ffn", x, (mix_a, mix_b, w_out), specs, mod, g, w13, w2, widx, tm)


def _pool_mix_ffn(x, g_mix, pool_w, pool_b, pool_scale, mod, g, w13, w2, widx, tm):
    b, s, d = x.shape
    per = tm // POOL_HALO
    last = s // POOL_HALO - 1
    row = pl.BlockSpec((1, d), lambda bi, i: (0, 0))
    specs = [
        pl.BlockSpec((None, POOL_HALO, d), lambda bi, i: (bi, jnp.maximum(i * per - 1, 0), 0)),
        pl.BlockSpec((None, POOL_HALO, d), lambda bi, i: (bi, jnp.minimum((i + 1) * per, last), 0)),
        row,
        pl.BlockSpec(pool_w.shape, lambda bi, i: (0, 0, 0)),
        row,
        row,
    ]
    inputs = (x, x, g_mix.reshape(1, d), pool_w, pool_b.reshape(1, d), pool_scale.reshape(1, d))
    return _ffn_call(functools.partial(_pool_mix_ffn_kernel, s_len=s), "pool_mix_ffn", x, inputs, specs,
                     mod, g, w13, w2, widx, tm)


def _proj_kernel(x_ref, mod_ref, g_ref, wt_ref, gain_ref, cos_ref, sin_ref,
                 qa_ref, ka_ref, va_ref, qb_ref, kb_ref, vb_ref):
    tm = x_ref.shape[0]
    h = _norm_mod(x_ref[...], g_ref[...], mod_ref[3:4, :], mod_ref[4:5, :]).astype(BF16)
    cos = cos_ref[...]
    sin = sin_ref[...]
    width = B_HEADS * HEAD_DIM

    def group_t(gi):
        return lax.dot_general(wt_ref[gi * width:(gi + 1) * width, :], h,
                               (((1,), (1,)), ((), ())), preferred_element_type=F32)

    def head_norm(ut, gain):
        x3 = ut.reshape(B_HEADS, HEAD_DIM, tm)
        ms = jnp.mean(x3 * x3, axis=1, keepdims=True)
        return (x3 * lax.rsqrt(ms + EPS)) * gain[None]

    def rope(x3):
        x1 = x3[:, :HEAD_DIM // 2, :]
        x2 = x3[:, HEAD_DIM // 2:, :]
        return jnp.concatenate([x1 * cos - x2 * sin, x1 * sin + x2 * cos], axis=1)

    qa = rope(head_norm(group_t(0), gain_ref[0])) * QK_SCALE
    qa_ref[...] = qa.reshape(A_HEADS, A_VDIM, tm).astype(BF16)
    ka = rope(head_norm(group_t(1), gain_ref[1])).reshape(A_HEADS, A_VDIM, tm)
    for hh in range(A_HEADS):
        ka_ref[hh] = ka[hh].T.astype(BF16)
    qb = head_norm(group_t(3), gain_ref[2]) * QK_SCALE
    qb_ref[...] = qb.astype(BF16)
    kb = head_norm(group_t(4), gain_ref[3]).reshape(B_HEADS // 2, 2 * HEAD_DIM, tm)
    for hp in range(B_HEADS // 2):
        kb_ref[hp] = kb[hp].T.astype(BF16)
    va_ref[:, :A_VDIM, :] = group_t(2).reshape(A_HEADS, A_VDIM, tm).astype(BF16)
    va_ref[:, A_VDIM:, :] = jnp.ones((A_HEADS, DA_VROWS - A_VDIM, tm), BF16)
    vb_ref[:, :HEAD_DIM, :] = group_t(5).reshape(B_HEADS, HEAD_DIM, tm).astype(BF16)
    vb_ref[:, HEAD_DIM:, :] = jnp.ones((B_HEADS, NA_VROWS - HEAD_DIM, tm), BF16)


def _proj(x, mod, g, w_in_t, gains, cos_t, sin_t, tm):
    b, s, d = x.shape
    tok = lambda bi, i: (bi, 0, 0, i)
    seq = lambda bi, i: (bi, 0, i, 0)
    return pl.pallas_call(
        _proj_kernel,
        grid=(b, s // tm),
        in_specs=[
            pl.BlockSpec((None, tm, d), lambda bi, i: (bi, i, 0)),
            pl.BlockSpec((None, N_MOD, d), lambda bi, i: (bi, 0, 0)),
            pl.BlockSpec((1, d), lambda bi, i: (0, 0)),
            _resident(w_in_t.shape, lambda bi, i: (0, 0)),
            pl.BlockSpec((4, HEAD_DIM, tm), lambda bi, i: (0, 0, 0)),
            pl.BlockSpec((HEAD_DIM // 2, tm), lambda bi, i: (0, i)),
            pl.BlockSpec((HEAD_DIM // 2, tm), lambda bi, i: (0, i)),
        ],
        out_specs=[
            pl.BlockSpec((None, A_HEADS, A_VDIM, tm), tok),
            pl.BlockSpec((None, A_HEADS, tm, A_VDIM), seq),
            pl.BlockSpec((None, A_HEADS, DA_VROWS, tm), tok),
            pl.BlockSpec((None, B_HEADS, HEAD_DIM, tm), tok),
            pl.BlockSpec((None, B_HEADS // 2, tm, 2 * HEAD_DIM), seq),
            pl.BlockSpec((None, B_HEADS, NA_VROWS, tm), tok),
        ],
        out_shape=[
            jax.ShapeDtypeStruct((b, A_HEADS, A_VDIM, s), BF16),
            jax.ShapeDtypeStruct((b, A_HEADS, s, A_VDIM), BF16),
            jax.ShapeDtypeStruct((b, A_HEADS, DA_VROWS, s), BF16),
            jax.ShapeDtypeStruct((b, B_HEADS, HEAD_DIM, s), BF16),
            jax.ShapeDtypeStruct((b, B_HEADS // 2, s, 2 * HEAD_DIM), BF16),
            jax.ShapeDtypeStruct((b, B_HEADS, NA_VROWS, s), BF16),
        ],
        compiler_params=_params("parallel", "parallel"),
        name="qkv_proj",
    )(x, mod, g.reshape(1, d), w_in_t, gains, cos_t, sin_t)


def _dattn_kernel(q_ref, k_ref, v_ref, lam_ref, gn_ref, sg_ref, o_ref, acc_ref, *, kblk, lambda_init):
    s_len = k_ref.shape[0]
    n_strips = q_ref.shape[1] // DA_QSUB
    lp = lam_ref[...]
    lam = (jnp.exp(jnp.sum(lp[0:1] * lp[1:2], axis=-1, keepdims=True))
           - jnp.exp(jnp.sum(lp[2:3] * lp[3:4], axis=-1, keepdims=True)) + lambda_init)
    gmax = jnp.max(jnp.abs(gn_ref[...]), axis=-1, keepdims=True)
    shift = (DA_BOUND_MARGIN * HEAD_DIM * QK_SCALE) * gmax[0:1] * gmax[1:2]
    z = jnp.zeros((HEAD_DIM, DA_QSUB), BF16)

    def strip_rhs(sb):
        qt = q_ref[:, sb * DA_QSUB:(sb + 1) * DA_QSUB]
        return jnp.concatenate([jnp.concatenate([qt[:HEAD_DIM], z], axis=0),
                                jnp.concatenate([z, qt[HEAD_DIM:]], axis=0)], axis=1)

    def finish(sb, num, den):
        o = num / den
        ot = o[:, :DA_QSUB] - lam * o[:, DA_QSUB:]
        ms = jnp.mean(ot * ot, axis=0, keepdims=True)
        y = ((ot * lax.rsqrt(ms + EPS)) * sg_ref[...]) * (1.0 - lambda_init)
        o_ref[sb * DA_QSUB:(sb + 1) * DA_QSUB, :] = y.T.astype(BF16)

    min_sum = None
    for sb in range(n_strips):
        rhs = strip_rhs(sb)
        num = None
        den8 = None
        for k0 in range(0, s_len, kblk):
            st = jnp.dot(k_ref[k0:k0 + kblk, :], rhs, preferred_element_type=F32)
            p = jnp.exp(st - shift)
            part = jnp.sum(p.reshape(kblk // 8, 8, 2 * DA_QSUB), axis=0)
            pv = jnp.dot(v_ref[:A_VDIM, k0:k0 + kblk], p.astype(BF16), preferred_element_type=F32)
            num = pv if num is None else num + pv
            den8 = part if den8 is None else den8 + part
        den = jnp.sum(den8, axis=0, keepdims=True)
        finish(sb, num, den)
        min_sum = den if min_sum is None else jnp.minimum(min_sum, den)

    sums_ok = jnp.min(min_sum) >= DA_MIN_SUM

    @pl.when(jnp.logical_not(sums_ok))
    def _():
        for sb in range(n_strips):
            rhs = strip_rhs(sb)
            acc_ref[...] = jnp.zeros_like(acc_ref)

            def safe_body(t, m):
                k0 = pl.multiple_of(t * DA_SAFE_KBLK, DA_SAFE_KBLK)
                st = jnp.dot(k_ref[pl.ds(k0, DA_SAFE_KBLK), :], rhs, preferred_element_type=F32)
                m_new = jnp.maximum(m, jnp.max(st, axis=0, keepdims=True))
                pv = jnp.dot(v_ref[:, pl.ds(k0, DA_SAFE_KBLK)], jnp.exp(st - m_new).astype(BF16),
                             preferred_element_type=F32)
                acc_ref[...] = acc_ref[...] * jnp.exp(m - m_new) + pv
                return m_new

            lax.fori_loop(0, s_len // DA_SAFE_KBLK, safe_body, jnp.full((1, 2 * DA_QSUB), NEG_INF, F32))
            finish(sb, acc_ref[:A_VDIM, :], acc_ref[A_VDIM:A_VDIM + 1, :])


def _dattn(qa_t, ka, va_t, a_lambda, a_qk_gain, a_subln, lambda_init, qblk):
    b, nh, _, s = qa_t.shape
    assert qblk % DA_QSUB == 0
    return pl.pallas_call(
        functools.partial(_dattn_kernel, kblk=_tile(s, DA_KBLK), lambda_init=lambda_init),
        grid=(b, nh, s // qblk),
        in_specs=[
            pl.BlockSpec((None, None, A_VDIM, qblk), lambda bi, h, i: (bi, h, 0, i)),
            pl.BlockSpec((None, None, s, A_VDIM), lambda bi, h, i: (bi, h, 0, 0)),
            pl.BlockSpec((None, None, DA_VROWS, s), lambda bi, h, i: (bi, h, 0, 0)),
            pl.BlockSpec((4, HEAD_DIM), lambda bi, h, i: (0, 0)),
            pl.BlockSpec((2, HEAD_DIM), lambda bi, h, i: (0, 0)),
            pl.BlockSpec((A_VDIM, DA_QSUB), lambda bi, h, i: (0, 0)),
        ],
        out_specs=pl.BlockSpec((None, qblk, A_VDIM), lambda bi, h, i: (bi, i, h)),
        out_shape=jax.ShapeDtypeStruct((b, s, nh * A_VDIM), BF16),
        scratch_shapes=[pltpu.VMEM((DA_VROWS, 2 * DA_QSUB), F32)],
        compiler_params=_params("parallel", "parallel", "arbitrary"),
        name="diff_attn",
    )(qa_t, ka, va_t, a_lambda, a_qk_gain, jnp.broadcast_to(a_subln[:, None], (A_VDIM, DA_QSUB)))


def _toeplitz_selector():
    kc = np.arange(GRID_W)[:, None]
    qc = np.arange(GRID_W)[None, :]
    cs = np.clip(qc - NA_COLS // 2, 0, GRID_W - NA_COLS)
    col_ok = (kc >= cs) & (kc < cs + NA_COLS)
    dc = np.clip(kc - qc + NA_COLS - 1, 0, 2 * NA_COLS - 2)
    sel = (np.arange(2 * NA_COLS)[:, None, None] == dc[None]) & col_ok[None]
    mask = np.where(col_ok, 0.0, NEG_INF)
    return (sel.reshape(2 * NA_COLS, GRID_W * GRID_W).astype(np.float32),
            mask.reshape(1, GRID_W * GRID_W).astype(np.float32))


def _rpb_expand_kernel(r_ref, gn_ref, sel_ref, mask_ref, o_ref, *, ndr, ndc):
    r = r_ref[...]
    nh = r.shape[0] // NA_DR_PAD
    rows = lax.broadcasted_iota(jnp.int32, r.shape, 0)
    cols = lax.broadcasted_iota(jnp.int32, r.shape, 1)
    real = ((rows & (NA_DR_PAD - 1)) < ndr) & (cols < ndc)
    row_max = jnp.max(jnp.where(real, r, NEG_INF), axis=1, keepdims=True)
    head_max = jnp.max(row_max.reshape(nh, NA_DR_PAD, 1), axis=1, keepdims=True)
    gmax = jnp.max(jnp.abs(gn_ref[...]), axis=-1, keepdims=True)
    bound = (DA_BOUND_MARGIN * HEAD_DIM * QK_SCALE) * gmax[0:1] * gmax[1:2]
    shift = jnp.broadcast_to(head_max + bound[None], (nh, NA_DR_PAD, 1)).reshape(r.shape[0], 1)
    o_ref[...] = (jnp.dot(r, sel_ref[...], preferred_element_type=F32, precision=lax.Precision.HIGHEST)
                  + mask_ref[...]) - shift


def _na_bias_table(rpb, qk_gain):
    nh, ndr, ndc = rpb.shape
    sel, mask = _toeplitz_selector()
    r = jnp.pad(rpb, ((0, 0), (0, NA_DR_PAD - ndr), (0, sel.shape[0] - ndc))).reshape(nh * NA_DR_PAD, sel.shape[0])
    flat = pl.pallas_call(
        functools.partial(_rpb_expand_kernel, ndr=ndr, ndc=ndc),
        out_shape=jax.ShapeDtypeStruct((nh * NA_DR_PAD, GRID_W * GRID_W), F32),
        name="rpb_expand",
    )(r, qk_gain, sel, mask)
    tile = flat.reshape(nh, NA_DR_PAD, GRID_W, GRID_W)
    neg = jnp.full((nh, GRID_W, GRID_W), NEG_INF, F32)
    half = NA_ROWS // 2
    kinds = []
    for kind in range(3):
        win_rows = []
        for i in range(NA_WIN_ROWS):
            blocks = []
            for j in range(NA_BLOCK_ROWS):
                if kind == 0:
                    dr, ok = i - j + NA_ROWS - 1, i < NA_ROWS
                elif kind == 1:
                    dr, ok = i - j + half - 1, j <= i < j + NA_ROWS
                else:
                    dr, ok = i - j - 1, i >= NA_WIN_ROWS - NA_ROWS
                blocks.append(tile[:, dr] if ok else neg)
            win_rows.append(jnp.concatenate(blocks, axis=-1))
        kinds.append(jnp.concatenate(win_rows, axis=1))
    return jnp.stack(kinds)


def _na_kernel(q_ref, k_ref, v_ref, bias_ref, o_ref, *, nblk):
    qn = NA_BLOCK_ROWS * GRID_W
    win = NA_WIN_ROWS * GRID_W
    nh = q_ref.shape[0]
    rb = pl.program_id(1)
    t0 = pl.multiple_of(jnp.clip(rb - 1, 0, nblk - 3) * qn, qn)
    kind = jnp.where(rb == 0, 0, jnp.where(rb == nblk - 1, 2, 1))
    z = jnp.zeros((HEAD_DIM, qn), BF16)

    def logits(h):
        qt = q_ref[h]
        rhs = jnp.concatenate([qt, z] if h % 2 == 0 else [z, qt], axis=0)
        kwin = k_ref[h // 2, pl.ds(t0, win), :]
        return jnp.dot(kwin, rhs, preferred_element_type=F32) + bias_ref[kind, h]

    def weighted(h, p):
        return jnp.dot(v_ref[h, :, pl.ds(t0, win)], p.astype(BF16), preferred_element_type=F32)

    def normalised(pvs):
        outs = [pv[:HEAD_DIM] / pv[HEAD_DIM:HEAD_DIM + 1] for pv in pvs]
        return jnp.concatenate(outs, axis=0).T.astype(BF16)

    sts = [logits(h) for h in range(nh)]
    pvs = [weighted(h, jnp.exp(st)) for h, st in enumerate(sts)]
    o_ref[...] = normalised(pvs)

    sums = functools.reduce(jnp.minimum, [pv[HEAD_DIM:HEAD_DIM + 1] for pv in pvs])
    sums_ok = jnp.min(sums) >= NA_MIN_SUM

    @pl.when(jnp.logical_not(sums_ok))
    def _():
        safe = []
        for h in range(nh):
            st = logits(h)
            safe.append(weighted(h, jnp.exp(st - jnp.max(st, axis=0, keepdims=True))))
        o_ref[...] = normalised(safe)


def _na(qb_t, kb, vb_t, bias_tab):
    b, nh, _, s = qb_t.shape
    qn = NA_BLOCK_ROWS * GRID_W
    nblk = s // qn
    assert nblk >= 3
    return pl.pallas_call(
        functools.partial(_na_kernel, nblk=nblk),
        grid=(b, nblk),
        in_specs=[
            pl.BlockSpec((None, nh, HEAD_DIM, qn), lambda bi, rb: (bi, 0, 0, rb)),
            _resident((None, nh // 2, s, 2 * HEAD_DIM), lambda bi, rb: (bi, 0, 0, 0)),
            _resident((None, nh, NA_VROWS, s), lambda bi, rb: (bi, 0, 0, 0)),
            _resident(bias_tab.shape, lambda bi, rb: (0, 0, 0, 0)),
        ],
        out_specs=pl.BlockSpec((None, qn, nh * HEAD_DIM), lambda bi, rb: (bi, rb, 0)),
        out_shape=jax.ShapeDtypeStruct((b, s, nh * HEAD_DIM), BF16),
        compiler_params=_params("parallel", "arbitrary"),
        name="nbr_attn",
    )(qb_t, kb, vb_t, bias_tab)


def _tile(s, want):
    return want if s % want == 0 else s


def kernel(x, c, ada_w, ada_b, norm_g, ff_w13, ff_w2, w_in, w_out, a_qk_norm, a_lambda, a_subln,
           b_qk_norm, b_rpb, pool_w, pool_b, pool_scale):
    b, s, d = x.shape
    depth = ada_w.shape[0]
    tm = _tile(s, TOKEN_TILE)
    ffn_tm = _tile(s, FFN_TOKEN_TILE)
    mod = _adaln(c, ada_w, ada_b)

    pos = jnp.arange(s, dtype=F32)
    inv = ROPE_THETA ** (-jnp.arange(0, HEAD_DIM, 2, dtype=F32) / HEAD_DIM)
    ang = inv[:, None] * pos[None, :]
    cos_t, sin_t = jnp.cos(ang), jnp.sin(ang)

    w13 = ff_w13.astype(BF16)
    w2 = ff_w2.astype(BF16)
    for layer in range(depth):
        m = mod[layer]
        x = _ffn(x, m, norm_g[layer, 0], w13, w2, (layer, 0), ffn_tm)
        tail = (m, norm_g[layer, 2], w13, w2, (layer, 1), ffn_tm)
        if layer % 2 == 0:
            e = layer // 2
            lambda_init = 0.8 - 0.6 * math.exp(-0.3 * layer)
            gains = jnp.stack([a_qk_norm[e, 0], a_qk_norm[e, 1], b_qk_norm[e, 0], b_qk_norm[e, 1]])
            gains = jnp.broadcast_to(gains[:, :, None], (4, HEAD_DIM, tm))
            qa_t, ka, va_t, qb_t, kb, vb_t = _proj(x, m, norm_g[layer, 1], w_in[e].T.astype(BF16),
                                                   gains, cos_t, sin_t, tm)
            mix_a = _dattn(qa_t, ka, va_t, a_lambda[e], a_qk_norm[e], a_subln[e], lambda_init,
                           _tile(s, DA_QBLK))
            mix_b = _na(qb_t, kb, vb_t, _na_bias_table(b_rpb[e], b_qk_norm[e]))
            x = _attn_mix_ffn(x, mix_a, mix_b, w_out[e].astype(BF16), *tail)
        else:
            o = layer // 2
            x = _pool_mix_ffn(x, norm_g[layer, 1], pool_w[o].astype(BF16), pool_b[o], pool_scale[o], *tail)
    return x
```

```python
import functools
import math

import numpy as np
import jax
import jax.numpy as jnp
from jax import lax
from jax.experimental import pallas as pl
from jax.experimental.pallas import tpu as pltpu

F32 = jnp.float32
BF16 = jnp.bfloat16

D_MODEL = 1024
HEAD_DIM = 64
A_HEADS = 4
A_VDIM = 2 * HEAD_DIM
B_HEADS = 8
D_FF = 2816
GRID_W = 64
NA_ROWS = 8
NA_COLS = 16
POOL_WINDOWS = (2, 4, 8, 16)
POOL_GROUP_DIM = D_MODEL // len(POOL_WINDOWS)
ROPE_THETA = 10000.0
EPS = 1e-6
N_MOD = 9
NEG_INF = -1e30
QK_SCALE = HEAD_DIM ** -0.5

V7X_VMEM_LIMIT_BYTES = 56 * 1024 * 1024
BF16_SUBLANES = 16
ADALN_K_TILE = 256
TOKEN_TILE = 512
FFN_TOKEN_TILE = 1024
FFN_ROW_BLOCK = 256
FF_CHUNK = 256
DA_QBLK = 1024
DA_QSUB = 256
DA_KBLK = 4096
DA_BOUND_MARGIN = 1.02
DA_MIN_SUM = 1e-18
DA_SAFE_KBLK = 256
DA_VROWS = A_VDIM + BF16_SUBLANES
NA_BLOCK_ROWS = 4
NA_WIN_ROWS = 3 * NA_BLOCK_ROWS
NA_VROWS = HEAD_DIM + BF16_SUBLANES
NA_DR_PAD = 16
NA_MIN_SUM = 1e-18
POOL_HALO = 8


def _params(*sem):
    return pltpu.CompilerParams(dimension_semantics=sem, vmem_limit_bytes=V7X_VMEM_LIMIT_BYTES)


def _resident(shape, index_map):
    return pl.BlockSpec(shape, index_map, pipeline_mode=pl.Buffered(1))


def _norm_mod(x, g, shift, scale):
    ms = jnp.mean(x * x, axis=-1, keepdims=True)
    y = (x * lax.rsqrt(ms + EPS)) * g
    return y * (1.0 + scale) + shift


def _adaln_kernel(c_ref, w_ref, b_ref, o_ref):
    tk = w_ref.shape[0]
    k = pl.program_id(1)
    c = c_ref[:, pl.ds(pl.multiple_of(k * tk, tk), tk)]
    cond = c / (1.0 + jnp.exp(-c))

    @pl.when(k == 0)
    def _():
        o_ref[...] = jnp.broadcast_to(b_ref[...], o_ref.shape)

    o_ref[...] += jnp.dot(cond.astype(BF16), w_ref[...].astype(BF16), preferred_element_type=F32)


def _adaln(c, ada_w, ada_b):
    depth, d, n = ada_w.shape
    b = c.shape[0]
    bp = -(-b // 8) * 8
    cp = jnp.pad(c, ((0, bp - b), (0, 0)))
    tk = ADALN_K_TILE
    out = pl.pallas_call(
        _adaln_kernel,
        grid=(depth, d // tk),
        in_specs=[
            pl.BlockSpec((bp, d), lambda l, k: (0, 0)),
            pl.BlockSpec((None, tk, n), lambda l, k: (l, k, 0)),
            pl.BlockSpec((None, 1, n), lambda l, k: (l, 0, 0)),
        ],
        out_specs=pl.BlockSpec((None, bp, n), lambda l, k: (l, 0, 0)),
        out_shape=jax.ShapeDtypeStruct((depth, bp, n), F32),
        compiler_params=_params("arbitrary", "arbitrary"),
        name="adaln",
    )(cp, ada_w, ada_b.reshape(depth, 1, n))
    return out[:, :b].reshape(depth, b, N_MOD, d)


def _swiglu_tile(prep, x_rows, tm, mod_ref, g_ref, w13_ref, w2_ref, o_ref, act_ref, h_buf, row0):
    rows = FFN_ROW_BLOCK if tm % FFN_ROW_BLOCK == 0 else tm
    n_blocks = tm // rows
    n_chunks = D_FF // FF_CHUNK
    gate = 0.5 * mod_ref[row0 + 2:row0 + 3, :]

    def norm_step(k, r0):
        def run():
            h_buf[k % 2] = _norm_mod(x_rows(k, r0, rows), g_ref[...], mod_ref[row0:row0 + 1, :],
                                     mod_ref[row0 + 1:row0 + 2, :]).astype(BF16)
        return run

    steps = [list(prep(k, k * rows, rows)) + [norm_step(k, k * rows)] for k in range(n_blocks)]
    for step in steps[0]:
        step()
    for k in range(n_blocks):
        r0 = k * rows
        side = steps[k + 1] if k + 1 < n_blocks else []
        after = {((j + 1) * n_chunks) // (len(side) + 1) - 1: step for j, step in enumerate(side)}
        assert len(after) == len(side)
        h = h_buf[k % 2]
        for ci in range(n_chunks):
            lo = ci * FF_CHUNK
            a = jnp.dot(h, w13_ref[:, lo:lo + FF_CHUNK], preferred_element_type=F32)
            b = jnp.dot(h, w13_ref[:, D_FF + lo:D_FF + lo + FF_CHUNK], preferred_element_type=F32)
            act_ref[r0:r0 + rows, lo:lo + FF_CHUNK] = ((a / (1.0 + jnp.exp(-a))) * b).astype(BF16)
            if ci in after:
                after[ci]()
        y = jnp.dot(act_ref[r0:r0 + rows, :], w2_ref[...], preferred_element_type=F32)
        o_ref[r0:r0 + rows, :] = x_rows(k, r0, rows) + gate * y


def _ffn_kernel(x_ref, mod_ref, g_ref, w13_ref, w2_ref, o_ref, act_ref, h_buf):
    _swiglu_tile(lambda k, r0, rows: (), lambda k, r0, rows: x_ref[r0:r0 + rows, :], x_ref.shape[0],
                 mod_ref, g_ref, w13_ref, w2_ref, o_ref, act_ref, h_buf, 0)


def _attn_mix_ffn_kernel(x_ref, ma_ref, mb_ref, wo_ref, mod_ref, g_ref, w13_ref, w2_ref, o_ref,
                         act_ref, h_buf, x1_buf):
    half = ma_ref.shape[1]

    def prep(k, r0, rows):
        def mix():
            y = (jnp.dot(ma_ref[r0:r0 + rows, :], wo_ref[:half, :], preferred_element_type=F32)
                 + jnp.dot(mb_ref[r0:r0 + rows, :], wo_ref[half:, :], preferred_element_type=F32))
            x1_buf[k % 2] = x_ref[r0:r0 + rows, :] + mod_ref[5:6, :] * y
        return [mix]

    _swiglu_tile(prep, lambda k, r0, rows: x1_buf[k % 2], x_ref.shape[0],
                 mod_ref, g_ref, w13_ref, w2_ref, o_ref, act_ref, h_buf, 6)


def _pool_mix_ffn_kernel(x_ref, xp_ref, xn_ref, gm_ref, pw_ref, pb_ref, ps_ref, mod_ref, g_ref, w13_ref, w2_ref,
                         o_ref, act_ref, h_buf, x1_buf, *, s_len):
    tm = x_ref.shape[0]
    i = pl.program_id(1)

    def count(t, w):
        return (jnp.minimum(t + w // 2, s_len) - jnp.maximum(t - w // 2, 0)).astype(F32)

    def prep(k, r0, rows):
        if r0 == 0:
            head, head_on = xp_ref, (i > 0).astype(F32)
        else:
            head, head_on = x_ref.at[r0 - POOL_HALO:r0, :], 1.0
        if r0 + rows == tm:
            tail, tail_on = xn_ref, (i < pl.num_programs(1) - 1).astype(F32)
        else:
            tail, tail_on = x_ref.at[r0 + rows:r0 + rows + POOL_HALO, :], 1.0
        body = x_ref.at[r0:r0 + rows, :]
        inv_rms = {}

        def stats():
            for name, ref in (("head", head), ("body", body), ("tail", tail)):
                xs = ref[...]
                inv_rms[name] = lax.rsqrt(jnp.mean(xs * xs, axis=-1, keepdims=True) + EPS)

        def group_step(gi, w):
            def run():
                cols = slice(gi * POOL_GROUP_DIM, (gi + 1) * POOL_GROUP_DIM)
                gain, shift, scale = gm_ref[:, cols], mod_ref[3:4, cols], mod_ref[4:5, cols]

                def part(name, ref):
                    return ((ref[:, cols] * inv_rms[name]) * gain) * (1.0 + scale) + shift

                h = part("body", body)
                run_sum = jnp.concatenate([part("head", head) * head_on, h, part("tail", tail) * tail_on], axis=0)
                span = 1
                while span < w:
                    n = run_sum.shape[0]
                    run_sum = run_sum[:n - span] + run_sum[span:]
                    span *= 2
                start = POOL_HALO - w // 2
                seg = run_sum[start:start + rows]
                t_head = i * tm + r0 + lax.broadcasted_iota(jnp.int32, (POOL_HALO, 1), 0)
                t_tail = t_head + (rows - POOL_HALO)
                pooled = jnp.concatenate([seg[:POOL_HALO] / count(t_head, w),
                                          seg[POOL_HALO:rows - POOL_HALO] * (1.0 / w),
                                          seg[rows - POOL_HALO:] / count(t_tail, w)], axis=0)
                y = jnp.dot((pooled - h).astype(BF16), pw_ref[gi], preferred_element_type=F32)
                y = (y + pb_ref[:, cols]) * ps_ref[:, cols]
                x1_buf[k % 2, :, cols] = body[:, cols] + mod_ref[5:6, cols] * y
            return run

        return [stats] + [group_step(gi, w) for gi, w in enumerate(POOL_WINDOWS)]

    _swiglu_tile(prep, lambda k, r0, rows: x1_buf[k % 2], tm,
                 mod_ref, g_ref, w13_ref, w2_ref, o_ref, act_ref, h_buf, 6)


def _ffn_call(body, name, x, mixer_inputs, mixer_specs, mod, g, w13, w2, widx, tm):
    b, s, d = x.shape
    rows = FFN_ROW_BLOCK if tm % FFN_ROW_BLOCK == 0 else tm
    row = pl.BlockSpec((1, d), lambda bi, i: (0, 0))
    scratch = [pltpu.VMEM((tm, D_FF), BF16), pltpu.VMEM((2, rows, d), BF16)]
    if mixer_inputs:
        scratch.append(pltpu.VMEM((2, rows, d), F32))
    return pl.pallas_call(
        body,
        grid=(b, s // tm),
        in_specs=[pl.BlockSpec((None, tm, d), lambda bi, i: (bi, i, 0))] + list(mixer_specs) + [
            pl.BlockSpec((None, N_MOD, d), lambda bi, i: (bi, 0, 0)),
            row,
            _resident((None, None, d, 2 * D_FF), lambda bi, i: (*widx, 0, 0)),
            _resident((None, None, D_FF, d), lambda bi, i: (*widx, 0, 0)),
        ],
        out_specs=pl.BlockSpec((None, tm, d), lambda bi, i: (bi, i, 0)),
        out_shape=jax.ShapeDtypeStruct(x.shape, F32),
        scratch_shapes=scratch,
        compiler_params=_params("parallel", "parallel"),
        name=name,
    )(x, *mixer_inputs, mod, g.reshape(1, d), w13, w2)


def _ffn(x, mod, g, w13, w2, widx, tm):
    return _ffn_call(_ffn_kernel, "ffn", x, (), (), mod, g, w13, w2, widx, tm)


def _attn_mix_ffn(x, mix_a, mix_b, w_out, mod, g, w13, w2, widx, tm):
    half = mix_a.shape[-1]
    mix = pl.BlockSpec((None, tm, half), lambda bi, i: (bi, i, 0))
    specs = [mix, mix, _resident(w_out.shape, lambda bi, i: (0, 0))]
    return _ffn_call(_attn_mix_ffn_kernel, "attn_mix_ffn", x, (mix_a, mix_b, w_out), specs, mod, g, w13, w2, widx, tm)


def _pool_mix_ffn(x, g_mix, pool_w, pool_b, pool_scale, mod, g, w13, w2, widx, tm):
    b, s, d = x.shape
    per = tm // POOL_HALO
    last = s // POOL_HALO - 1
    row = pl.BlockSpec((1, d), lambda bi, i: (0, 0))
    specs = [
        pl.BlockSpec((None, POOL_HALO, d), lambda bi, i: (bi, jnp.maximum(i * per - 1, 0), 0)),
        pl.BlockSpec((None, POOL_HALO, d), lambda bi, i: (bi, jnp.minimum((i + 1) * per, last), 0)),
        row,
        pl.BlockSpec(pool_w.shape, lambda bi, i: (0, 0, 0)),
        row,
        row,
    ]
    inputs = (x, x, g_mix.reshape(1, d), pool_w, pool_b.reshape(1, d), pool_scale.reshape(1, d))
    return _ffn_call(functools.partial(_pool_mix_ffn_kernel, s_len=s), "pool_mix_ffn", x, inputs, specs,
                     mod, g, w13, w2, widx, tm)


def _proj_kernel(x_ref, mod_ref, g_ref, wt_ref, gain_ref, cos_ref, sin_ref,
                 qa_ref, ka_ref, va_ref, qb_ref, kb_ref, vb_ref):
    tm = x_ref.shape[0]
    h = _norm_mod(x_ref[...], g_ref[...], mod_ref[3:4, :], mod_ref[4:5, :]).astype(BF16)
    cos = cos_ref[...]
    sin = sin_ref[...]
    width = B_HEADS * HEAD_DIM

    def group_t(gi):
        return lax.dot_general(wt_ref[gi * width:(gi + 1) * width, :], h,
                               (((1,), (1,)), ((), ())), preferred_element_type=F32)

    def head_norm(ut, gain):
        x3 = ut.reshape(B_HEADS, HEAD_DIM, tm)
        ms = jnp.mean(x3 * x3, axis=1, keepdims=True)
        return (x3 * lax.rsqrt(ms + EPS)) * gain[None]

    def rope(x3):
        x1 = x3[:, :HEAD_DIM // 2, :]
        x2 = x3[:, HEAD_DIM // 2:, :]
        return jnp.concatenate([x1 * cos - x2 * sin, x1 * sin + x2 * cos], axis=1)

    qa = rope(head_norm(group_t(0), gain_ref[0])) * QK_SCALE
    qa_ref[...] = qa.reshape(A_HEADS, A_VDIM, tm).astype(BF16)
    ka = rope(head_norm(group_t(1), gain_ref[1])).reshape(A_HEADS, A_VDIM, tm)
    for hh in range(A_HEADS):
        ka_ref[hh] = ka[hh].T.astype(BF16)
    qb = head_norm(group_t(3), gain_ref[2]) * QK_SCALE
    qb_ref[...] = qb.astype(BF16)
    kb = head_norm(group_t(4), gain_ref[3]).reshape(B_HEADS // 2, 2 * HEAD_DIM, tm)
    for hp in range(B_HEADS // 2):
        kb_ref[hp] = kb[hp].T.astype(BF16)
    va_ref[:, :A_VDIM, :] = group_t(2).reshape(A_HEADS, A_VDIM, tm).astype(BF16)
    va_ref[:, A_VDIM:, :] = jnp.ones((A_HEADS, DA_VROWS - A_VDIM, tm), BF16)
    vb_ref[:, :HEAD_DIM, :] = group_t(5).reshape(B_HEADS, HEAD_DIM, tm).astype(BF16)
    vb_ref[:, HEAD_DIM:, :] = jnp.ones((B_HEADS, NA_VROWS - HEAD_DIM, tm), BF16)


def _proj(x, mod, g, w_in_t, gains, cos_t, sin_t, tm):
    b, s, d = x.shape
    tok = lambda bi, i: (bi, 0, 0, i)
    seq = lambda bi, i: (bi, 0, i, 0)
    return pl.pallas_call(
        _proj_kernel,
        grid=(b, s // tm),
        in_specs=[
            pl.BlockSpec((None, tm, d), lambda bi, i: (bi, i, 0)),
            pl.BlockSpec((None, N_MOD, d), lambda bi, i: (bi, 0, 0)),
            pl.BlockSpec((1, d), lambda bi, i: (0, 0)),
            _resident(w_in_t.shape, lambda bi, i: (0, 0)),
            pl.BlockSpec((4, HEAD_DIM, tm), lambda bi, i: (0, 0, 0)),
            pl.BlockSpec((HEAD_DIM // 2, tm), lambda bi, i: (0, i)),
            pl.BlockSpec((HEAD_DIM // 2, tm), lambda bi, i: (0, i)),
        ],
        out_specs=[
            pl.BlockSpec((None, A_HEADS, A_VDIM, tm), tok),
            pl.BlockSpec((None, A_HEADS, tm, A_VDIM), seq),
            pl.BlockSpec((None, A_HEADS, DA_VROWS, tm), tok),
            pl.BlockSpec((None, B_HEADS, HEAD_DIM, tm), tok),
            pl.BlockSpec((None, B_HEADS // 2, tm, 2 * HEAD_DIM), seq),
            pl.BlockSpec((None, B_HEADS, NA_VROWS, tm), tok),
        ],
        out_shape=[
            jax.ShapeDtypeStruct((b, A_HEADS, A_VDIM, s), BF16),
            jax.ShapeDtypeStruct((b, A_HEADS, s, A_VDIM), BF16),
            jax.ShapeDtypeStruct((b, A_HEADS, DA_VROWS, s), BF16),
            jax.ShapeDtypeStruct((b, B_HEADS, HEAD_DIM, s), BF16),
            jax.ShapeDtypeStruct((b, B_HEADS // 2, s, 2 * HEAD_DIM), BF16),
            jax.ShapeDtypeStruct((b, B_HEADS, NA_VROWS, s), BF16),
        ],
        compiler_params=_params("parallel", "parallel"),
        name="qkv_proj",
    )(x, mod, g.reshape(1, d), w_in_t, gains, cos_t, sin_t)


def _dattn_kernel(q_ref, k_ref, v_ref, lam_ref, gn_ref, sg_ref, o_ref, acc_ref, *, kblk, lambda_init):
    s_len = k_ref.shape[0]
    n_strips = q_ref.shape[1] // DA_QSUB
    lp = lam_ref[...]
    lam = (jnp.exp(jnp.sum(lp[0:1] * lp[1:2], axis=-1, keepdims=True))
           - jnp.exp(jnp.sum(lp[2:3] * lp[3:4], axis=-1, keepdims=True)) + lambda_init)
    gmax = jnp.max(jnp.abs(gn_ref[...]), axis=-1, keepdims=True)
    shift = (DA_BOUND_MARGIN * HEAD_DIM * QK_SCALE) * gmax[0:1] * gmax[1:2]
    z = jnp.zeros((HEAD_DIM, DA_QSUB), BF16)

    def strip_rhs(sb):
        qt = q_ref[:, sb * DA_QSUB:(sb + 1) * DA_QSUB]
        return jnp.concatenate([jnp.concatenate([qt[:HEAD_DIM], z], axis=0),
                                jnp.concatenate([z, qt[HEAD_DIM:]], axis=0)], axis=1)

    def finish(sb, num, den):
        o = num / den
        ot = o[:, :DA_QSUB] - lam * o[:, DA_QSUB:]
        ms = jnp.mean(ot * ot, axis=0, keepdims=True)
        y = ((ot * lax.rsqrt(ms + EPS)) * sg_ref[...]) * (1.0 - lambda_init)
        o_ref[sb * DA_QSUB:(sb + 1) * DA_QSUB, :] = y.T.astype(BF16)

    min_sum = None
    for sb in range(n_strips):
        rhs = strip_rhs(sb)
        num = None
        den8 = None
        for k0 in range(0, s_len, kblk):
            st = jnp.dot(k_ref[k0:k0 + kblk, :], rhs, preferred_element_type=F32)
            p = jnp.exp(st - shift)
            part = jnp.sum(p.reshape(kblk // 8, 8, 2 * DA_QSUB), axis=0)
            pv = jnp.dot(v_ref[:A_VDIM, k0:k0 + kblk], p.astype(BF16), preferred_element_type=F32)
            num = pv if num is None else num + pv
            den8 = part if den8 is None else den8 + part
        den = jnp.sum(den8, axis=0, keepdims=True)
        finish(sb, num, den)
        min_sum = den if min_sum is None else jnp.minimum(min_sum, den)

    sums_ok = jnp.min(min_sum) >= DA_MIN_SUM

    @pl.when(jnp.logical_not(sums_ok))
    def _():
        for sb in range(n_strips):
            rhs = strip_rhs(sb)
            acc_ref[...] = jnp.zeros_like(acc_ref)

            def safe_body(t, m):
                k0 = pl.multiple_of(t * DA_SAFE_KBLK, DA_SAFE_KBLK)
                st = jnp.dot(k_ref[pl.ds(k0, DA_SAFE_KBLK), :], rhs, preferred_element_type=F32)
                m_new = jnp.maximum(m, jnp.max(st, axis=0, keepdims=True))
                pv = jnp.dot(v_ref[:, pl.ds(k0, DA_SAFE_KBLK)], jnp.exp(st - m_new).astype(BF16),
                             preferred_element_type=F32)
                acc_ref[...] = acc_ref[...] * jnp.exp(m - m_new) + pv
                return m_new

            lax.fori_loop(0, s_len // DA_SAFE_KBLK, safe_body, jnp.full((1, 2 * DA_QSUB), NEG_INF, F32))
            finish(sb, acc_ref[:A_VDIM, :], acc_ref[A_VDIM:A_VDIM + 1, :])


def _dattn(qa_t, ka, va_t, a_lambda, a_qk_gain, a_subln, lambda_init, qblk):
    b, nh, _, s = qa_t.shape
    assert qblk % DA_QSUB == 0
    return pl.pallas_call(
        functools.partial(_dattn_kernel, kblk=_tile(s, DA_KBLK), lambda_init=lambda_init),
        grid=(b, nh, s // qblk),
        in_specs=[
            pl.BlockSpec((None, None, A_VDIM, qblk), lambda bi, h, i: (bi, h, 0, i)),
            pl.BlockSpec((None, None, s, A_VDIM), lambda bi, h, i: (bi, h, 0, 0)),
            pl.BlockSpec((None, None, DA_VROWS, s), lambda bi, h, i: (bi, h, 0, 0)),
            pl.BlockSpec((4, HEAD_DIM), lambda bi, h, i: (0, 0)),
            pl.BlockSpec((2, HEAD_DIM), lambda bi, h, i: (0, 0)),
            pl.BlockSpec((A_VDIM, DA_QSUB), lambda bi, h, i: (0, 0)),
        ],
        out_specs=pl.BlockSpec((None, qblk, A_VDIM), lambda bi, h, i: (bi, i, h)),
        out_shape=jax.ShapeDtypeStruct((b, s, nh * A_VDIM), BF16),
        scratch_shapes=[pltpu.VMEM((DA_VROWS, 2 * DA_QSUB), F32)],
        compiler_params=_params("parallel", "parallel", "arbitrary"),
        name="diff_attn",
    )(qa_t, ka, va_t, a_lambda, a_qk_gain, jnp.broadcast_to(a_subln[:, None], (A_VDIM, DA_QSUB)))


def _toeplitz_selector():
    kc = np.arange(GRID_W)[:, None]
    qc = np.arange(GRID_W)[None, :]
    cs = np.clip(qc - NA_COLS // 2, 0, GRID_W - NA_COLS)
    col_ok = (kc >= cs) & (kc < cs + NA_COLS)
    dc = np.clip(kc - qc + NA_COLS - 1, 0, 2 * NA_COLS - 2)
    sel = (np.arange(2 * NA_COLS)[:, None, None] == dc[None]) & col_ok[None]
    mask = np.where(col_ok, 0.0, NEG_INF)
    return (sel.reshape(2 * NA_COLS, GRID_W * GRID_W).astype(np.float32),
            mask.reshape(1, GRID_W * GRID_W).astype(np.float32))


def _rpb_expand_kernel(r_ref, gn_ref, sel_ref, mask_ref, o_ref, *, ndr, ndc):
    r = r_ref[...]
    nh = r.shape[0] // NA_DR_PAD
    rows = lax.broadcasted_iota(jnp.int32, r.shape, 0)
    cols = lax.broadcasted_iota(jnp.int32, r.shape, 1)
    real = ((rows & (NA_DR_PAD - 1)) < ndr) & (cols < ndc)
    row_max = jnp.max(jnp.where(real, r, NEG_INF), axis=1, keepdims=True)
    head_max = jnp.max(row_max.reshape(nh, NA_DR_PAD, 1), axis=1, keepdims=True)
    gmax = jnp.max(jnp.abs(gn_ref[...]), axis=-1, keepdims=True)
    bound = (DA_BOUND_MARGIN * HEAD_DIM * QK_SCALE) * gmax[0:1] * gmax[1:2]
    shift = jnp.broadcast_to(head_max + bound[None], (nh, NA_DR_PAD, 1)).reshape(r.shape[0], 1)
    o_ref[...] = (jnp.dot(r, sel_ref[...], preferred_element_type=F32, precision=lax.Precision.HIGHEST)
                  + mask_ref[...]) - shift


def _na_bias_table(rpb, qk_gain):
    nh, ndr, ndc = rpb.shape
    sel, mask = _toeplitz_selector()
    r = jnp.pad(rpb, ((0, 0), (0, NA_DR_PAD - ndr), (0, sel.shape[0] - ndc))).reshape(nh * NA_DR_PAD, sel.shape[0])
    flat = pl.pallas_call(
        functools.partial(_rpb_expand_kernel, ndr=ndr, ndc=ndc),
        out_shape=jax.ShapeDtypeStruct((nh * NA_DR_PAD, GRID_W * GRID_W), F32),
        name="rpb_expand",
    )(r, qk_gain, sel, mask)
    tile = flat.reshape(nh, NA_DR_PAD, GRID_W, GRID_W)
    neg = jnp.full((nh, GRID_W, GRID_W), NEG_INF, F32)
    half = NA_ROWS // 2
    kinds = []
    for kind in range(3):
        win_rows = []
        for i in range(NA_WIN_ROWS):
            blocks = []
            for j in range(NA_BLOCK_ROWS):
                if kind == 0:
                    dr, ok = i - j + NA_ROWS - 1, i < NA_ROWS
                elif kind == 1:
                    dr, ok = i - j + half - 1, j <= i < j + NA_ROWS
                else:
                    dr, ok = i - j - 1, i >= NA_WIN_ROWS - NA_ROWS
                blocks.append(tile[:, dr] if ok else neg)
            win_rows.append(jnp.concatenate(blocks, axis=-1))
        kinds.append(jnp.concatenate(win_rows, axis=1))
    return jnp.stack(kinds)


def _na_kernel(q_ref, k_ref, v_ref, bias_ref, o_ref, *, nblk):
    qn = NA_BLOCK_ROWS * GRID_W
    win = NA_WIN_ROWS * GRID_W
    nh = q_ref.shape[0]
    rb = pl.program_id(1)
    t0 = pl.multiple_of(jnp.clip(rb - 1, 0, nblk - 3) * qn, qn)
    kind = jnp.where(rb == 0, 0, jnp.where(rb == nblk - 1, 2, 1))
    z = jnp.zeros((HEAD_DIM, qn), BF16)

    def logits(h):
        qt = q_ref[h]
        rhs = jnp.concatenate([qt, z] if h % 2 == 0 else [z, qt], axis=0)
        kwin = k_ref[h // 2, pl.ds(t0, win), :]
        return jnp.dot(kwin, rhs, preferred_element_type=F32) + bias_ref[kind, h]

    def weighted(h, p):
        return jnp.dot(v_ref[h, :, pl.ds(t0, win)], p.astype(BF16), preferred_element_type=F32)

    def normalised(pvs):
        outs = [pv[:HEAD_DIM] / pv[HEAD_DIM:HEAD_DIM + 1] for pv in pvs]
        return jnp.concatenate(outs, axis=0).T.astype(BF16)

    sts = [logits(h) for h in range(nh)]
    pvs = [weighted(h, jnp.exp(st)) for h, st in enumerate(sts)]
    o_ref[...] = normalised(pvs)

    sums = functools.reduce(jnp.minimum, [pv[HEAD_DIM:HEAD_DIM + 1] for pv in pvs])
    sums_ok = jnp.min(sums) >= NA_MIN_SUM

    @pl.when(jnp.logical_not(sums_ok))
    def _():
        safe = []
        for h in range(nh):
            st = logits(h)
            safe.append(weighted(h, jnp.exp(st - jnp.max(st, axis=0, keepdims=True))))
        o_ref[...] = normalised(safe)


def _na(qb_t, kb, vb_t, bias_tab):
    b, nh, _, s = qb_t.shape
    qn = NA_BLOCK_ROWS * GRID_W
    nblk = s // qn
    assert nblk >= 3
    return pl.pallas_call(
        functools.partial(_na_kernel, nblk=nblk),
        grid=(b, nblk),
        in_specs=[
            pl.BlockSpec((None, nh, HEAD_DIM, qn), lambda bi, rb: (bi, 0, 0, rb)),
            _resident((None, nh // 2, s, 2 * HEAD_DIM), lambda bi, rb: (bi, 0, 0, 0)),
            _resident((None, nh, NA_VROWS, s), lambda bi, rb: (bi, 0, 0, 0)),
            _resident(bias_tab.shape, lambda bi, rb: (0, 0, 0, 0)),
        ],
        out_specs=pl.BlockSpec((None, qn, nh * HEAD_DIM), lambda bi, rb: (bi, rb, 0)),
        out_shape=jax.ShapeDtypeStruct((b, s, nh * HEAD_DIM), BF16),
        compiler_params=_params("parallel", "arbitrary"),
        name="nbr_attn",
    )(qb_t, kb, vb_t, bias_tab)


def _tile(s, want):
    return want if s % want == 0 else s


def kernel(x, c, ada_w, ada_b, norm_g, ff_w13, ff_w2, w_in, w_out, a_qk_norm, a_lambda, a_subln,
           b_qk_norm, b_rpb, pool_w, pool_b, pool_scale):
    b, s, d = x.shape
    depth = ada_w.shape[0]
    tm = _tile(s, TOKEN_TILE)
    ffn_tm = _tile(s, FFN_TOKEN_TILE)
    mod = _adaln(c, ada_w, ada_b)

    pos = jnp.arange(s, dtype=F32)
    inv = ROPE_THETA ** (-jnp.arange(0, HEAD_DIM, 2, dtype=F32) / HEAD_DIM)
    ang = inv[:, None] * pos[None, :]
    cos_t, sin_t = jnp.cos(ang), jnp.sin(ang)

    w13 = ff_w13.astype(BF16)
    w2 = ff_w2.astype(BF16)
    for layer in range(depth):
        m = mod[layer]
        x = _ffn(x, m, norm_g[layer, 0], w13, w2, (layer, 0), ffn_tm)
        tail = (m, norm_g[layer, 2], w13, w2, (layer, 1), ffn_tm)
        if layer % 2 == 0:
            e = layer // 2
            lambda_init = 0.8 - 0.6 * math.exp(-0.3 * layer)
            gains = jnp.stack([a_qk_norm[e, 0], a_qk_norm[e, 1], b_qk_norm[e, 0], b_qk_norm[e, 1]])
            gains = jnp.broadcast_to(gains[:, :, None], (4, HEAD_DIM, tm))
            qa_t, ka, va_t, qb_t, kb, vb_t = _proj(x, m, norm_g[layer, 1], w_in[e].T.astype(BF16),
                                                   gains, cos_t, sin_t, tm)
            mix_a = _dattn(qa_t, ka, va_t, a_lambda[e], a_qk_norm[e], a_subln[e], lambda_init,
                           _tile(s, DA_QBLK))
            mix_b = _na(qb_t, kb, vb_t, _na_bias_table(b_rpb[e], b_qk_norm[e]))
            x = _attn_mix_ffn(x, mix_a, mix_b, w_out[e].astype(BF16), *tail)
        else:
            o = layer // 2
            x = _pool_mix_ffn(x, norm_g[layer, 1], pool_w[o].astype(BF16), pool_b[o], pool_scale[o], *tail)
    return x
```

```python
import functools
import math

import numpy as np
import jax
import jax.numpy as jnp
from jax import lax
from jax.experimental import pallas as pl
from jax.experimental.pallas import tpu as pltpu

F32 = jnp.float32
BF16 = jnp.bfloat16

D_MODEL = 1024
HEAD_DIM = 64
A_HEADS = 4
A_VDIM = 2 * HEAD_DIM
B_HEADS = 8
D_FF = 2816
GRID_W = 64
NA_ROWS = 8
NA_COLS = 16
POOL_WINDOWS = (2, 4, 8, 16)
POOL_GROUP_DIM = D_MODEL // len(POOL_WINDOWS)
ROPE_THETA = 10000.0
EPS = 1e-6
N_MOD = 9
NEG_INF = -1e30
QK_SCALE = HEAD_DIM ** -0.5

V7X_VMEM_LIMIT_BYTES = 56 * 1024 * 1024
BF16_SUBLANES = 16
ADALN_K_TILE = 256
TOKEN_TILE = 512
PROJ_TOKEN_SUB = 256
FFN_TOKEN_TILE = 1024
FFN_ROW_BLOCK = 256
FF_CHUNK = 256
DA_QBLK = 1024
DA_QSUB = 256
DA_KBLK = 4096
DA_BOUND_MARGIN = 1.02
DA_MIN_SUM = 1e-18
DA_SAFE_KBLK = 256
DA_VROWS = A_VDIM + BF16_SUBLANES
NA_BLOCK_ROWS = 4
NA_WIN_ROWS = 3 * NA_BLOCK_ROWS
NA_VROWS = HEAD_DIM + BF16_SUBLANES
NA_DR_PAD = 16
NA_MIN_SUM = 1e-18
POOL_HALO = 8


def _params(*sem):
    return pltpu.CompilerParams(dimension_semantics=sem, vmem_limit_bytes=V7X_VMEM_LIMIT_BYTES)


def _resident(shape, index_map):
    return pl.BlockSpec(shape, index_map, pipeline_mode=pl.Buffered(1))


def _norm_mod(x, g, shift, scale):
    ms = jnp.mean(x * x, axis=-1, keepdims=True)
    y = (x * lax.rsqrt(ms + EPS)) * g
    return y * (1.0 + scale) + shift


def _adaln_kernel(c_ref, w_ref, b_ref, o_ref):
    tk = w_ref.shape[0]
    k = pl.program_id(1)
    c = c_ref[:, pl.ds(pl.multiple_of(k * tk, tk), tk)]
    cond = c / (1.0 + jnp.exp(-c))

    @pl.when(k == 0)
    def _():
        o_ref[...] = jnp.broadcast_to(b_ref[...], o_ref.shape)

    o_ref[...] += jnp.dot(cond.astype(BF16), w_ref[...].astype(BF16), preferred_element_type=F32)


def _adaln(c, ada_w, ada_b):
    depth, d, n = ada_w.shape
    b = c.shape[0]
    bp = -(-b // 8) * 8
    cp = jnp.pad(c, ((0, bp - b), (0, 0)))
    tk = ADALN_K_TILE
    out = pl.pallas_call(
        _adaln_kernel,
        grid=(depth, d // tk),
        in_specs=[
            pl.BlockSpec((bp, d), lambda l, k: (0, 0)),
            pl.BlockSpec((None, tk, n), lambda l, k: (l, k, 0)),
            pl.BlockSpec((None, 1, n), lambda l, k: (l, 0, 0)),
        ],
        out_specs=pl.BlockSpec((None, bp, n), lambda l, k: (l, 0, 0)),
        out_shape=jax.ShapeDtypeStruct((depth, bp, n), F32),
        compiler_params=_params("arbitrary", "arbitrary"),
        name="adaln",
    )(cp, ada_w, ada_b.reshape(depth, 1, n))
    return out[:, :b].reshape(depth, b, N_MOD, d)


def _swiglu_tile(prep, x_rows, tm, mod_ref, g_ref, w13_ref, w2_ref, o_ref, act_ref, h_buf, row0):
    rows = FFN_ROW_BLOCK if tm % FFN_ROW_BLOCK == 0 else tm
    n_blocks = tm // rows
    n_chunks = D_FF // FF_CHUNK
    gate = 0.5 * mod_ref[row0 + 2:row0 + 3, :]

    def norm_step(k, r0):
        def run():
            h_buf[k % 2] = _norm_mod(x_rows(k, r0, rows), g_ref[...], mod_ref[row0:row0 + 1, :],
                                     mod_ref[row0 + 1:row0 + 2, :]).astype(BF16)
        return run

    steps = [list(prep(k, k * rows, rows)) + [norm_step(k, k * rows)] for k in range(n_blocks)]
    for step in steps[0]:
        step()
    for k in range(n_blocks):
        r0 = k * rows
        side = steps[k + 1] if k + 1 < n_blocks else []
        after = {((j + 1) * n_chunks) // (len(side) + 1) - 1: step for j, step in enumerate(side)}
        assert len(after) == len(side)
        h = h_buf[k % 2]
        for ci in range(n_chunks):
            lo = ci * FF_CHUNK
            a = jnp.dot(h, w13_ref[:, lo:lo + FF_CHUNK], preferred_element_type=F32)
            b = jnp.dot(h, w13_ref[:, D_FF + lo:D_FF + lo + FF_CHUNK], preferred_element_type=F32)
            act_ref[r0:r0 + rows, lo:lo + FF_CHUNK] = ((a / (1.0 + jnp.exp(-a))) * b).astype(BF16)
            if ci in after:
                after[ci]()
        y = jnp.dot(act_ref[r0:r0 + rows, :], w2_ref[...], preferred_element_type=F32)
        o_ref[r0:r0 + rows, :] = x_rows(k, r0, rows) + gate * y


def _ffn_kernel(x_ref, mod_ref, g_ref, w13_ref, w2_ref, o_ref, act_ref, h_buf):
    _swiglu_tile(lambda k, r0, rows: (), lambda k, r0, rows: x_ref[r0:r0 + rows, :], x_ref.shape[0],
                 mod_ref, g_ref, w13_ref, w2_ref, o_ref, act_ref, h_buf, 0)


def _attn_mix_ffn_kernel(x_ref, ma_ref, mb_ref, wo_ref, mod_ref, g_ref, w13_ref, w2_ref, o_ref,
                         act_ref, h_buf, x1_buf):
    half = ma_ref.shape[1]

    def prep(k, r0, rows):
        def mix():
            y = (jnp.dot(ma_ref[r0:r0 + rows, :], wo_ref[:half, :], preferred_element_type=F32)
                 + jnp.dot(mb_ref[r0:r0 + rows, :], wo_ref[half:, :], preferred_element_type=F32))
            x1_buf[k % 2] = x_ref[r0:r0 + rows, :] + mod_ref[5:6, :] * y
        return [mix]

    _swiglu_tile(prep, lambda k, r0, rows: x1_buf[k % 2], x_ref.shape[0],
                 mod_ref, g_ref, w13_ref, w2_ref, o_ref, act_ref, h_buf, 6)


def _pool_mix_ffn_kernel(x_ref, xp_ref, xn_ref, gm_ref, pw_ref, pb_ref, ps_ref, mod_ref, g_ref, w13_ref, w2_ref,
                         o_ref, act_ref, h_buf, x1_buf, *, s_len):
    tm = x_ref.shape[0]
    i = pl.program_id(1)

    def count(t, w):
        return (jnp.minimum(t + w // 2, s_len) - jnp.maximum(t - w // 2, 0)).astype(F32)

    def prep(k, r0, rows):
        if r0 == 0:
            head, head_on = xp_ref, (i > 0).astype(F32)
        else:
            head, head_on = x_ref.at[r0 - POOL_HALO:r0, :], 1.0
        if r0 + rows == tm:
            tail, tail_on = xn_ref, (i < pl.num_programs(1) - 1).astype(F32)
        else:
            tail, tail_on = x_ref.at[r0 + rows:r0 + rows + POOL_HALO, :], 1.0
        body = x_ref.at[r0:r0 + rows, :]
        inv_rms = {}

        def stats():
            for name, ref in (("head", head), ("body", body), ("tail", tail)):
                xs = ref[...]
                inv_rms[name] = lax.rsqrt(jnp.mean(xs * xs, axis=-1, keepdims=True) + EPS)

        def group_step(gi, w):
            def run():
                cols = slice(gi * POOL_GROUP_DIM, (gi + 1) * POOL_GROUP_DIM)
                gain, shift, scale = gm_ref[:, cols], mod_ref[3:4, cols], mod_ref[4:5, cols]

                def part(name, ref):
                    return ((ref[:, cols] * inv_rms[name]) * gain) * (1.0 + scale) + shift

                h = part("body", body)
                run_sum = jnp.concatenate([part("head", head) * head_on, h, part("tail", tail) * tail_on], axis=0)
                span = 1
                while span < w:
                    n = run_sum.shape[0]
                    run_sum = run_sum[:n - span] + run_sum[span:]
                    span *= 2
                start = POOL_HALO - w // 2
                seg = run_sum[start:start + rows]
                t_head = i * tm + r0 + lax.broadcasted_iota(jnp.int32, (POOL_HALO, 1), 0)
                t_tail = t_head + (rows - POOL_HALO)
                pooled = jnp.concatenate([seg[:POOL_HALO] / count(t_head, w),
                                          seg[POOL_HALO:rows - POOL_HALO] * (1.0 / w),
                                          seg[rows - POOL_HALO:] / count(t_tail, w)], axis=0)
                y = jnp.dot((pooled - h).astype(BF16), pw_ref[gi], preferred_element_type=F32)
                y = (y + pb_ref[:, cols]) * ps_ref[:, cols]
                x1_buf[k % 2, :, cols] = body[:, cols] + mod_ref[5:6, cols] * y
            return run

        return [stats] + [group_step(gi, w) for gi, w in enumerate(POOL_WINDOWS)]

    _swiglu_tile(prep, lambda k, r0, rows: x1_buf[k % 2], tm,
                 mod_ref, g_ref, w13_ref, w2_ref, o_ref, act_ref, h_buf, 6)


def _ffn_call(body, name, x, mixer_inputs, mixer_specs, mod, g, w13, w2, widx, tm):
    b, s, d = x.shape
    rows = FFN_ROW_BLOCK if tm % FFN_ROW_BLOCK == 0 else tm
    row = pl.BlockSpec((1, d), lambda bi, i: (0, 0))
    scratch = [pltpu.VMEM((tm, D_FF), BF16), pltpu.VMEM((2, rows, d), BF16)]
    if mixer_inputs:
        scratch.append(pltpu.VMEM((2, rows, d), F32))
    return pl.pallas_call(
        body,
        grid=(b, s // tm),
        in_specs=[pl.BlockSpec((None, tm, d), lambda bi, i: (bi, i, 0))] + list(mixer_specs) + [
            pl.BlockSpec((None, N_MOD, d), lambda bi, i: (bi, 0, 0)),
            row,
            _resident((None, d, 2 * D_FF), lambda bi, i: (widx, 0, 0)),
            _resident((None, D_FF, d), lambda bi, i: (widx, 0, 0)),
        ],
        out_specs=pl.BlockSpec((None, tm, d), lambda bi, i: (bi, i, 0)),
        out_shape=jax.ShapeDtypeStruct(x.shape, F32),
        scratch_shapes=scratch,
        compiler_params=_params("parallel", "parallel"),
        name=name,
    )(x, *mixer_inputs, mod, g.reshape(1, d), w13, w2)


def _ffn(x, mod, g, w13, w2, widx, tm):
    return _ffn_call(_ffn_kernel, "ffn", x, (), (), mod, g, w13, w2, widx, tm)


def _attn_mix_ffn(x, mix_a, mix_b, w_out, mod, g, w13, w2, widx, tm):
    half = mix_a.shape[-1]
    mix = pl.BlockSpec((None, tm, half), lambda bi, i: (bi, i, 0))
    specs = [mix, mix, _resident(w_out.shape, lambda bi, i: (0, 0))]
    return _ffn_call(_attn_mix_ffn_kernel, "attn_mix_ffn", x, (mix_a, mix_b, w_out), specs, mod, g, w13, w2, widx, tm)


def _pool_mix_ffn(x, g_mix, pool_w, pool_b, pool_scale, mod, g, w13, w2, widx, tm):
    b, s, d = x.shape
    per = tm // POOL_HALO
    last = s // POOL_HALO - 1
    row = pl.BlockSpec((1, d), lambda bi, i: (0, 0))
    specs = [
        pl.BlockSpec((None, POOL_HALO, d), lambda bi, i: (bi, jnp.maximum(i * per - 1, 0), 0)),
        pl.BlockSpec((None, POOL_HALO, d), lambda bi, i: (bi, jnp.minimum((i + 1) * per, last), 0)),
        row,
        pl.BlockSpec(pool_w.shape, lambda bi, i: (0, 0, 0)),
        row,
        row,
    ]
    inputs = (x, x, g_mix.reshape(1, d), pool_w, pool_b.reshape(1, d), pool_scale.reshape(1, d))
    return _ffn_call(functools.partial(_pool_mix_ffn_kernel, s_len=s), "pool_mix_ffn", x, inputs, specs,
                     mod, g, w13, w2, widx, tm)


def _proj_kernel(x_ref, mod_ref, g_ref, wt_ref, gain_ref, cos_ref, sin_ref,
                 qa_ref, ka_ref, va_ref, qb_ref, kb_ref, vb_ref, h_buf):
    tm = x_ref.shape[0]
    ts = PROJ_TOKEN_SUB if tm % PROJ_TOKEN_SUB == 0 else tm
    width = B_HEADS * HEAD_DIM

    def norm_step(j):
        h_buf[j % 2] = _norm_mod(x_ref[j * ts:(j + 1) * ts, :], g_ref[...], mod_ref[3:4, :],
                                 mod_ref[4:5, :]).astype(BF16)

    norm_step(0)
    for j in range(tm // ts):
        tok = slice(j * ts, (j + 1) * ts)
        h = h_buf[j % 2]
        cos = cos_ref[:, tok]
        sin = sin_ref[:, tok]

        def group_t(gi):
            return lax.dot_general(wt_ref[gi * width:(gi + 1) * width, :], h,
                                   (((1,), (1,)), ((), ())), preferred_element_type=F32)

        def head_norm(ut, gain):
            x3 = ut.reshape(B_HEADS, HEAD_DIM, ts)
            ms = jnp.mean(x3 * x3, axis=1, keepdims=True)
            return (x3 * lax.rsqrt(ms + EPS)) * gain[None]

        def rope(x3):
            x1 = x3[:, :HEAD_DIM // 2, :]
            x2 = x3[:, HEAD_DIM // 2:, :]
            return jnp.concatenate([x1 * cos - x2 * sin, x1 * sin + x2 * cos], axis=1)

        qa = rope(head_norm(group_t(0), gain_ref[0, :, tok])) * QK_SCALE
        qa_ref[:, :, tok] = qa.reshape(A_HEADS, A_VDIM, ts).astype(BF16)
        ka = rope(head_norm(group_t(1), gain_ref[1, :, tok])).reshape(A_HEADS, A_VDIM, ts)
        for hh in range(A_HEADS):
            ka_ref[hh, tok, :] = ka[hh].T.astype(BF16)
        if (j + 1) * ts < tm:
            norm_step(j + 1)
        qb = head_norm(group_t(3), gain_ref[2, :, tok]) * QK_SCALE
        qb_ref[:, :, tok] = qb.astype(BF16)
        kb = head_norm(group_t(4), gain_ref[3, :, tok]).reshape(B_HEADS // 2, 2 * HEAD_DIM, ts)
        for hp in range(B_HEADS // 2):
            kb_ref[hp, tok, :] = kb[hp].T.astype(BF16)
        va_ref[:, :A_VDIM, tok] = group_t(2).reshape(A_HEADS, A_VDIM, ts).astype(BF16)
        va_ref[:, A_VDIM:, tok] = jnp.ones((A_HEADS, DA_VROWS - A_VDIM, ts), BF16)
        vb_ref[:, :HEAD_DIM, tok] = group_t(5).reshape(B_HEADS, HEAD_DIM, ts).astype(BF16)
        vb_ref[:, HEAD_DIM:, tok] = jnp.ones((B_HEADS, NA_VROWS - HEAD_DIM, ts), BF16)


def _proj(x, mod, g, w_in_t, gains, cos_t, sin_t, tm):
    b, s, d = x.shape
    tok = lambda bi, i: (bi, 0, 0, i)
    seq = lambda bi, i: (bi, 0, i, 0)
    return pl.pallas_call(
        _proj_kernel,
        grid=(b, s // tm),
        in_specs=[
            pl.BlockSpec((None, tm, d), lambda bi, i: (bi, i, 0)),
            pl.BlockSpec((None, N_MOD, d), lambda bi, i: (bi, 0, 0)),
            pl.BlockSpec((1, d), lambda bi, i: (0, 0)),
            _resident(w_in_t.shape, lambda bi, i: (0, 0)),
            pl.BlockSpec((4, HEAD_DIM, tm), lambda bi, i: (0, 0, 0)),
            pl.BlockSpec((HEAD_DIM // 2, tm), lambda bi, i: (0, i)),
            pl.BlockSpec((HEAD_DIM // 2, tm), lambda bi, i: (0, i)),
        ],
        out_specs=[
            pl.BlockSpec((None, A_HEADS, A_VDIM, tm), tok),
            pl.BlockSpec((None, A_HEADS, tm, A_VDIM), seq),
            pl.BlockSpec((None, A_HEADS, DA_VROWS, tm), tok),
            pl.BlockSpec((None, B_HEADS, HEAD_DIM, tm), tok),
            pl.BlockSpec((None, B_HEADS // 2, tm, 2 * HEAD_DIM), seq),
            pl.BlockSpec((None, B_HEADS, NA_VROWS, tm), tok),
        ],
        out_shape=[
            jax.ShapeDtypeStruct((b, A_HEADS, A_VDIM, s), BF16),
            jax.ShapeDtypeStruct((b, A_HEADS, s, A_VDIM), BF16),
            jax.ShapeDtypeStruct((b, A_HEADS, DA_VROWS, s), BF16),
            jax.ShapeDtypeStruct((b, B_HEADS, HEAD_DIM, s), BF16),
            jax.ShapeDtypeStruct((b, B_HEADS // 2, s, 2 * HEAD_DIM), BF16),
            jax.ShapeDtypeStruct((b, B_HEADS, NA_VROWS, s), BF16),
        ],
        scratch_shapes=[pltpu.VMEM((2, PROJ_TOKEN_SUB if tm % PROJ_TOKEN_SUB == 0 else tm, d), BF16)],
        compiler_params=_params("parallel", "parallel"),
        name="qkv_proj",
    )(x, mod, g.reshape(1, d), w_in_t, gains, cos_t, sin_t)


def _dattn_kernel(q_ref, k_ref, v_ref, lam_ref, gn_ref, sg_ref, w13f_ref, w2f_ref, o_ref, w13b_ref, w2b_ref,
                  acc_ref, *, kblk, lambda_init):
    s_len = k_ref.shape[0]
    n_strips = q_ref.shape[1] // DA_QSUB
    lp = lam_ref[...]
    lam = (jnp.exp(jnp.sum(lp[0:1] * lp[1:2], axis=-1, keepdims=True))
           - jnp.exp(jnp.sum(lp[2:3] * lp[3:4], axis=-1, keepdims=True)) + lambda_init)
    gmax = jnp.max(jnp.abs(gn_ref[...]), axis=-1, keepdims=True)
    shift = (DA_BOUND_MARGIN * HEAD_DIM * QK_SCALE) * gmax[0:1] * gmax[1:2]
    z = jnp.zeros((HEAD_DIM, DA_QSUB), BF16)

    def strip_rhs(sb):
        qt = q_ref[:, sb * DA_QSUB:(sb + 1) * DA_QSUB]
        return jnp.concatenate([jnp.concatenate([qt[:HEAD_DIM], z], axis=0),
                                jnp.concatenate([z, qt[HEAD_DIM:]], axis=0)], axis=1)

    def finish(sb, num, den):
        o = num / den
        ot = o[:, :DA_QSUB] - lam * o[:, DA_QSUB:]
        ms = jnp.mean(ot * ot, axis=0, keepdims=True)
        y = ((ot * lax.rsqrt(ms + EPS)) * sg_ref[...]) * (1.0 - lambda_init)
        o_ref[sb * DA_QSUB:(sb + 1) * DA_QSUB, :] = y.T.astype(BF16)

    min_sum = None
    for sb in range(n_strips):
        if sb == n_strips // 2:
            w13b_ref[...] = w13f_ref[...].astype(BF16)
            w2b_ref[...] = w2f_ref[...].astype(BF16)
        rhs = strip_rhs(sb)
        num = None
        den8 = None
        for k0 in range(0, s_len, kblk):
            st = jnp.dot(k_ref[k0:k0 + kblk, :], rhs, preferred_element_type=F32)
            p = jnp.exp(st - shift)
            part = jnp.sum(p.reshape(kblk // 8, 8, 2 * DA_QSUB), axis=0)
            pv = jnp.dot(v_ref[:A_VDIM, k0:k0 + kblk], p.astype(BF16), preferred_element_type=F32)
            num = pv if num is None else num + pv
            den8 = part if den8 is None else den8 + part
        den = jnp.sum(den8, axis=0, keepdims=True)
        finish(sb, num, den)
        min_sum = den if min_sum is None else jnp.minimum(min_sum, den)

    sums_ok = jnp.min(min_sum) >= DA_MIN_SUM

    @pl.when(jnp.logical_not(sums_ok))
    def _():
        for sb in range(n_strips):
            rhs = strip_rhs(sb)
            acc_ref[...] = jnp.zeros_like(acc_ref)

            def safe_body(t, m):
                k0 = pl.multiple_of(t * DA_SAFE_KBLK, DA_SAFE_KBLK)
                st = jnp.dot(k_ref[pl.ds(k0, DA_SAFE_KBLK), :], rhs, preferred_element_type=F32)
                m_new = jnp.maximum(m, jnp.max(st, axis=0, keepdims=True))
                pv = jnp.dot(v_ref[:, pl.ds(k0, DA_SAFE_KBLK)], jnp.exp(st - m_new).astype(BF16),
                             preferred_element_type=F32)
                acc_ref[...] = acc_ref[...] * jnp.exp(m - m_new) + pv
                return m_new

            lax.fori_loop(0, s_len // DA_SAFE_KBLK, safe_body, jnp.full((1, 2 * DA_QSUB), NEG_INF, F32))
            finish(sb, acc_ref[:A_VDIM, :], acc_ref[A_VDIM:A_VDIM + 1, :])


def _cast_rows(total, steps):
    return min(r for r in range(BF16_SUBLANES, total + 1, BF16_SUBLANES) if total % r == 0 and total // r <= steps)


def _dattn(qa_t, ka, va_t, a_lambda, a_qk_gain, a_subln, lambda_init, qblk, ff_w13, ff_w2):
    b, nh, _, s = qa_t.shape
    assert qblk % DA_QSUB == 0
    nq = s // qblk
    depth, halves, d, f2 = ff_w13.shape
    n_sets = depth * halves - 1
    per_set = (b * nh * nq) // n_sets
    assert per_set >= 1
    rows13, rows2 = _cast_rows(d, per_set), _cast_rows(f2 // 2, per_set)

    def cast_block(rows_total, rows):
        nblk = rows_total // rows

        def locate(bi, h, i):
            blk = jnp.minimum((bi * nh + h) * nq + i, n_sets * nblk - 1)
            return blk // nblk, blk % nblk

        def src(bi, h, i):
            st, rb = locate(bi, h, i)
            return ((st + 1) // halves, (st + 1) % halves, rb, 0)

        def dst(bi, h, i):
            st, rb = locate(bi, h, i)
            return (st, rb, 0)

        return src, dst

    src13, dst13 = cast_block(d, rows13)
    src2, dst2 = cast_block(f2 // 2, rows2)
    return pl.pallas_call(
        functools.partial(_dattn_kernel, kblk=_tile(s, DA_KBLK), lambda_init=lambda_init),
        grid=(b, nh, nq),
        in_specs=[
            pl.BlockSpec((None, None, A_VDIM, qblk), lambda bi, h, i: (bi, h, 0, i)),
            pl.BlockSpec((None, None, s, A_VDIM), lambda bi, h, i: (bi, h, 0, 0)),
            pl.BlockSpec((None, None, DA_VROWS, s), lambda bi, h, i: (bi, h, 0, 0)),
            pl.BlockSpec((4, HEAD_DIM), lambda bi, h, i: (0, 0)),
            pl.BlockSpec((2, HEAD_DIM), lambda bi, h, i: (0, 0)),
            pl.BlockSpec((A_VDIM, DA_QSUB), lambda bi, h, i: (0, 0)),
            pl.BlockSpec((None, None, rows13, f2), src13),
            pl.BlockSpec((None, None, rows2, d), src2),
        ],
        out_specs=[
            pl.BlockSpec((None, qblk, A_VDIM), lambda bi, h, i: (bi, i, h)),
            pl.BlockSpec((None, rows13, f2), dst13),
            pl.BlockSpec((None, rows2, d), dst2),
        ],
        out_shape=[
            jax.ShapeDtypeStruct((b, s, nh * A_VDIM), BF16),
            jax.ShapeDtypeStruct((n_sets, d, f2), BF16),
            jax.ShapeDtypeStruct((n_sets, f2 // 2, d), BF16),
        ],
        scratch_shapes=[pltpu.VMEM((DA_VROWS, 2 * DA_QSUB), F32)],
        compiler_params=_params("arbitrary", "arbitrary", "arbitrary"),
        name="diff_attn",
    )(qa_t, ka, va_t, a_lambda, a_qk_gain, jnp.broadcast_to(a_subln[:, None], (A_VDIM, DA_QSUB)), ff_w13, ff_w2)


def _toeplitz_selector():
    kc = np.arange(GRID_W)[:, None]
    qc = np.arange(GRID_W)[None, :]
    cs = np.clip(qc - NA_COLS // 2, 0, GRID_W - NA_COLS)
    col_ok = (kc >= cs) & (kc < cs + NA_COLS)
    dc = np.clip(kc - qc + NA_COLS - 1, 0, 2 * NA_COLS - 2)
    sel = (np.arange(2 * NA_COLS)[:, None, None] == dc[None]) & col_ok[None]
    mask = np.where(col_ok, 0.0, NEG_INF)
    return (sel.reshape(2 * NA_COLS, GRID_W * GRID_W).astype(np.float32),
            mask.reshape(1, GRID_W * GRID_W).astype(np.float32))


def _rpb_expand_kernel(r_ref, gn_ref, sel_ref, mask_ref, o_ref, *, ndr, ndc):
    r = r_ref[...]
    nh = r.shape[0] // NA_DR_PAD
    rows = lax.broadcasted_iota(jnp.int32, r.shape, 0)
    cols = lax.broadcasted_iota(jnp.int32, r.shape, 1)
    real = ((rows & (NA_DR_PAD - 1)) < ndr) & (cols < ndc)
    row_max = jnp.max(jnp.where(real, r, NEG_INF), axis=1, keepdims=True)
    head_max = jnp.max(row_max.reshape(nh, NA_DR_PAD, 1), axis=1, keepdims=True)
    gmax = jnp.max(jnp.abs(gn_ref[...]), axis=-1, keepdims=True)
    bound = (DA_BOUND_MARGIN * HEAD_DIM * QK_SCALE) * gmax[0:1] * gmax[1:2]
    shift = jnp.broadcast_to(head_max + bound[None], (nh, NA_DR_PAD, 1)).reshape(r.shape[0], 1)
    o_ref[...] = (jnp.dot(r, sel_ref[...], preferred_element_type=F32, precision=lax.Precision.HIGHEST)
                  + mask_ref[...]) - shift


def _na_bias_table(rpb, qk_gain):
    nh, ndr, ndc = rpb.shape
    sel, mask = _toeplitz_selector()
    r = jnp.pad(rpb, ((0, 0), (0, NA_DR_PAD - ndr), (0, sel.shape[0] - ndc))).reshape(nh * NA_DR_PAD, sel.shape[0])
    flat = pl.pallas_call(
        functools.partial(_rpb_expand_kernel, ndr=ndr, ndc=ndc),
        out_shape=jax.ShapeDtypeStruct((nh * NA_DR_PAD, GRID_W * GRID_W), F32),
        name="rpb_expand",
    )(r, qk_gain, sel, mask)
    tile = flat.reshape(nh, NA_DR_PAD, GRID_W, GRID_W)
    neg = jnp.full((nh, GRID_W, GRID_W), NEG_INF, F32)
    half = NA_ROWS // 2
    kinds = []
    for kind in range(3):
        win_rows = []
        for i in range(NA_WIN_ROWS):
            blocks = []
            for j in range(NA_BLOCK_ROWS):
                if kind == 0:
                    dr, ok = i - j + NA_ROWS - 1, i < NA_ROWS
                elif kind == 1:
                    dr, ok = i - j + half - 1, j <= i < j + NA_ROWS
                else:
                    dr, ok = i - j - 1, i >= NA_WIN_ROWS - NA_ROWS
                blocks.append(tile[:, dr] if ok else neg)
            win_rows.append(jnp.concatenate(blocks, axis=-1))
        kinds.append(jnp.concatenate(win_rows, axis=1))
    return jnp.stack(kinds)


def _na_kernel(q_ref, k_ref, v_ref, bias_ref, o_ref, *, nblk):
    qn = NA_BLOCK_ROWS * GRID_W
    win = NA_WIN_ROWS * GRID_W
    nh = q_ref.shape[0]
    rb = pl.program_id(1)
    t0 = pl.multiple_of(jnp.clip(rb - 1, 0, nblk - 3) * qn, qn)
    kind = jnp.where(rb == 0, 0, jnp.where(rb == nblk - 1, 2, 1))
    z = jnp.zeros((HEAD_DIM, qn), BF16)

    def logits(h):
        qt = q_ref[h]
        rhs = jnp.concatenate([qt, z] if h % 2 == 0 else [z, qt], axis=0)
        kwin = k_ref[h // 2, pl.ds(t0, win), :]
        return jnp.dot(kwin, rhs, preferred_element_type=F32) + bias_ref[kind, h]

    def weighted(h, p):
        return jnp.dot(v_ref[h, :, pl.ds(t0, win)], p.astype(BF16), preferred_element_type=F32)

    def normalised(pvs):
        outs = [pv[:HEAD_DIM] / pv[HEAD_DIM:HEAD_DIM + 1] for pv in pvs]
        return jnp.concatenate(outs, axis=0).T.astype(BF16)

    sts = [logits(h) for h in range(nh)]
    pvs = [weighted(h, jnp.exp(st)) for h, st in enumerate(sts)]
    o_ref[...] = normalised(pvs)

    sums = functools.reduce(jnp.minimum, [pv[HEAD_DIM:HEAD_DIM + 1] for pv in pvs])
    sums_ok = jnp.min(sums) >= NA_MIN_SUM

    @pl.when(jnp.logical_not(sums_ok))
    def _():
        safe = []
        for h in range(nh):
            st = logits(h)
            safe.append(weighted(h, jnp.exp(st - jnp.max(st, axis=0, keepdims=True))))
        o_ref[...] = normalised(safe)


def _na(qb_t, kb, vb_t, bias_tab):
    b, nh, _, s = qb_t.shape
    qn = NA_BLOCK_ROWS * GRID_W
    nblk = s // qn
    assert nblk >= 3
    return pl.pallas_call(
        functools.partial(_na_kernel, nblk=nblk),
        grid=(b, nblk),
        in_specs=[
            pl.BlockSpec((None, nh, HEAD_DIM, qn), lambda bi, rb: (bi, 0, 0, rb)),
            _resident((None, nh // 2, s, 2 * HEAD_DIM), lambda bi, rb: (bi, 0, 0, 0)),
            _resident((None, nh, NA_VROWS, s), lambda bi, rb: (bi, 0, 0, 0)),
            _resident(bias_tab.shape, lambda bi, rb: (0, 0, 0, 0)),
        ],
        out_specs=pl.BlockSpec((None, qn, nh * HEAD_DIM), lambda bi, rb: (bi, rb, 0)),
        out_shape=jax.ShapeDtypeStruct((b, s, nh * HEAD_DIM), BF16),
        compiler_params=_params("parallel", "arbitrary"),
        name="nbr_attn",
    )(qb_t, kb, vb_t, bias_tab)


def _tile(s, want):
    return want if s % want == 0 else s


def kernel(x, c, ada_w, ada_b, norm_g, ff_w13, ff_w2, w_in, w_out, a_qk_norm, a_lambda, a_subln,
           b_qk_norm, b_rpb, pool_w, pool_b, pool_scale):
    b, s, d = x.shape
    depth = ada_w.shape[0]
    tm = _tile(s, TOKEN_TILE)
    ffn_tm = _tile(s, FFN_TOKEN_TILE)
    mod = _adaln(c, ada_w, ada_b)

    pos = jnp.arange(s, dtype=F32)
    inv = ROPE_THETA ** (-jnp.arange(0, HEAD_DIM, 2, dtype=F32) / HEAD_DIM)
    ang = inv[:, None] * pos[None, :]
    cos_t, sin_t = jnp.cos(ang), jnp.sin(ang)

    first = (ff_w13[0, :1].astype(BF16), ff_w2[0, :1].astype(BF16))
    rest = None

    def ffn_weights(layer, half):
        idx = 2 * layer + half
        return (*first, 0) if idx == 0 else (*rest, idx - 1)

    for layer in range(depth):
        m = mod[layer]
        x = _ffn(x, m, norm_g[layer, 0], *ffn_weights(layer, 0), ffn_tm)
        if layer % 2 == 0:
            e = layer // 2
            lambda_init = 0.8 - 0.6 * math.exp(-0.3 * layer)
            gains = jnp.stack([a_qk_norm[e, 0], a_qk_norm[e, 1], b_qk_norm[e, 0], b_qk_norm[e, 1]])
            gains = jnp.broadcast_to(gains[:, :, None], (4, HEAD_DIM, tm))
            qa_t, ka, va_t, qb_t, kb, vb_t = _proj(x, m, norm_g[layer, 1], w_in[e].T.astype(BF16),
                                                   gains, cos_t, sin_t, tm)
            mix_a, w13_rest, w2_rest = _dattn(qa_t, ka, va_t, a_lambda[e], a_qk_norm[e], a_subln[e], lambda_init,
                                              _tile(s, DA_QBLK), ff_w13, ff_w2)
            if rest is None:
                rest = (w13_rest, w2_rest)
            mix_b = _na(qb_t, kb, vb_t, _na_bias_table(b_rpb[e], b_qk_norm[e]))
            x = _attn_mix_ffn(x, mix_a, mix_b, w_out[e].astype(BF16), m, norm_g[layer, 2],
                              *ffn_weights(layer, 1), ffn_tm)
        else:
            o = layer // 2
            x = _pool_mix_ffn(x, norm_g[layer, 1], pool_w[o].astype(BF16), pool_b[o], pool_scale[o],
                              m, norm_g[layer, 2], *ffn_weights(layer, 1), ffn_tm)
    return x
```

```python
import functools
import math

import numpy as np
import jax
import jax.numpy as jnp
from jax import lax
from jax.experimental import pallas as pl
from jax.experimental.pallas import tpu as pltpu

F32 = jnp.float32
BF16 = jnp.bfloat16

D_MODEL = 1024
HEAD_DIM = 64
A_HEADS = 4
A_VDIM = 2 * HEAD_DIM
B_HEADS = 8
D_FF = 2816
GRID_W = 64
NA_ROWS = 8
NA_COLS = 16
POOL_WINDOWS = (2, 4, 8, 16)
POOL_GROUP_DIM = D_MODEL // len(POOL_WINDOWS)
ROPE_THETA = 10000.0
EPS = 1e-6
N_MOD = 9
NEG_INF = -1e30
QK_SCALE = HEAD_DIM ** -0.5

V7X_VMEM_LIMIT_BYTES = 56 * 1024 * 1024
BF16_SUBLANES = 16
ADALN_K_TILE = 256
TOKEN_TILE = 1024
PROJ_TOKEN_SUB = 256
FFN_TOKEN_TILE = 1024
FFN_ROW_BLOCK = 256
FF_CHUNK = 256
DA_QBLK = 2048
DA_QSUB = 256
DA_KBLK = 4096
DA_BOUND_MARGIN = 1.02
DA_MIN_SUM = 1e-18
DA_SAFE_KBLK = 256
DA_VROWS = A_VDIM + BF16_SUBLANES
NA_BLOCK_ROWS = 4
NA_WIN_ROWS = 3 * NA_BLOCK_ROWS
NA_STEP_BLOCKS = 2
NA_VROWS = HEAD_DIM + BF16_SUBLANES
NA_DR_PAD = 16
NA_MIN_SUM = 1e-18
POOL_HALO = 8


def _params(*sem):
    return pltpu.CompilerParams(dimension_semantics=sem, vmem_limit_bytes=V7X_VMEM_LIMIT_BYTES)


def _resident(shape, index_map):
    return pl.BlockSpec(shape, index_map, pipeline_mode=pl.Buffered(1))


def _norm_mod(x, g, shift, scale):
    ms = jnp.mean(x * x, axis=-1, keepdims=True)
    y = (x * lax.rsqrt(ms + EPS)) * g
    return y * (1.0 + scale) + shift


def _adaln_kernel(c_ref, w_ref, b_ref, o_ref):
    tk = w_ref.shape[0]
    k = pl.program_id(1)
    c = c_ref[:, pl.ds(pl.multiple_of(k * tk, tk), tk)]
    cond = c / (1.0 + jnp.exp(-c))

    @pl.when(k == 0)
    def _():
        o_ref[...] = jnp.broadcast_to(b_ref[...], o_ref.shape)

    o_ref[...] += jnp.dot(cond.astype(BF16), w_ref[...].astype(BF16), preferred_element_type=F32)


def _adaln(c, ada_w, ada_b):
    depth, d, n = ada_w.shape
    b = c.shape[0]
    bp = -(-b // 8) * 8
    cp = jnp.pad(c, ((0, bp - b), (0, 0)))
    tk = ADALN_K_TILE
    out = pl.pallas_call(
        _adaln_kernel,
        grid=(depth, d // tk),
        in_specs=[
            pl.BlockSpec((bp, d), lambda l, k: (0, 0)),
            pl.BlockSpec((None, tk, n), lambda l, k: (l, k, 0)),
            pl.BlockSpec((None, 1, n), lambda l, k: (l, 0, 0)),
        ],
        out_specs=pl.BlockSpec((None, bp, n), lambda l, k: (l, 0, 0)),
        out_shape=jax.ShapeDtypeStruct((depth, bp, n), F32),
        compiler_params=_params("arbitrary", "arbitrary"),
        name="adaln",
    )(cp, ada_w, ada_b.reshape(depth, 1, n))
    return out[:, :b].reshape(depth, b, N_MOD, d)


def _swiglu_tile(prep, x_rows, tm, mod_ref, g_ref, w13_ref, w2_ref, o_ref, act_ref, h_buf, row0):
    rows = FFN_ROW_BLOCK if tm % FFN_ROW_BLOCK == 0 else tm
    n_blocks = tm // rows
    n_chunks = D_FF // FF_CHUNK
    gate = 0.5 * mod_ref[row0 + 2:row0 + 3, :]

    def norm_step(k, r0):
        def run():
            h_buf[k % 2] = _norm_mod(x_rows(k, r0, rows), g_ref[...], mod_ref[row0:row0 + 1, :],
                                     mod_ref[row0 + 1:row0 + 2, :]).astype(BF16)
        return run

    steps = [list(prep(k, k * rows, rows)) + [norm_step(k, k * rows)] for k in range(n_blocks)]
    for step in steps[0]:
        step()
    for k in range(n_blocks):
        r0 = k * rows
        side = steps[k + 1] if k + 1 < n_blocks else []
        after = {((j + 1) * n_chunks) // (len(side) + 1) - 1: step for j, step in enumerate(side)}
        assert len(after) == len(side)
        h = h_buf[k % 2]
        for ci in range(n_chunks):
            lo = ci * FF_CHUNK
            a = jnp.dot(h, w13_ref[:, lo:lo + FF_CHUNK], preferred_element_type=F32)
            b = jnp.dot(h, w13_ref[:, D_FF + lo:D_FF + lo + FF_CHUNK], preferred_element_type=F32)
            act_ref[r0:r0 + rows, lo:lo + FF_CHUNK] = ((a / (1.0 + jnp.exp(-a))) * b).astype(BF16)
            if ci in after:
                after[ci]()
        y = jnp.dot(act_ref[r0:r0 + rows, :], w2_ref[...], preferred_element_type=F32)
        o_ref[r0:r0 + rows, :] = x_rows(k, r0, rows) + gate * y


def _ffn_kernel(x_ref, mod_ref, g_ref, w13_ref, w2_ref, o_ref, act_ref, h_buf):
    _swiglu_tile(lambda k, r0, rows: (), lambda k, r0, rows: x_ref[r0:r0 + rows, :], x_ref.shape[0],
                 mod_ref, g_ref, w13_ref, w2_ref, o_ref, act_ref, h_buf, 0)


def _attn_mix_ffn_kernel(x_ref, ma_ref, mb_ref, wo_ref, mod_ref, g_ref, w13_ref, w2_ref, o_ref,
                         act_ref, h_buf, x1_buf):
    half = ma_ref.shape[1]

    def prep(k, r0, rows):
        def mix():
            y = (jnp.dot(ma_ref[r0:r0 + rows, :], wo_ref[:half, :], preferred_element_type=F32)
                 + jnp.dot(mb_ref[r0:r0 + rows, :], wo_ref[half:, :], preferred_element_type=F32))
            x1_buf[k % 2] = x_ref[r0:r0 + rows, :] + mod_ref[5:6, :] * y
        return [mix]

    _swiglu_tile(prep, lambda k, r0, rows: x1_buf[k % 2], x_ref.shape[0],
                 mod_ref, g_ref, w13_ref, w2_ref, o_ref, act_ref, h_buf, 6)


def _pool_mix_ffn_kernel(x_ref, xp_ref, xn_ref, gm_ref, pw_ref, pb_ref, ps_ref, mod_ref, g_ref, w13_ref, w2_ref,
                         o_ref, act_ref, h_buf, x1_buf, *, s_len):
    tm = x_ref.shape[0]
    i = pl.program_id(1)

    def count(t, w):
        return (jnp.minimum(t + w // 2, s_len) - jnp.maximum(t - w // 2, 0)).astype(F32)

    def prep(k, r0, rows):
        if r0 == 0:
            head, head_on = xp_ref, (i > 0).astype(F32)
        else:
            head, head_on = x_ref.at[r0 - POOL_HALO:r0, :], 1.0
        if r0 + rows == tm:
            tail, tail_on = xn_ref, (i < pl.num_programs(1) - 1).astype(F32)
        else:
            tail, tail_on = x_ref.at[r0 + rows:r0 + rows + POOL_HALO, :], 1.0
        body = x_ref.at[r0:r0 + rows, :]
        inv_rms = {}

        def stats():
            for name, ref in (("head", head), ("body", body), ("tail", tail)):
                xs = ref[...]
                inv_rms[name] = lax.rsqrt(jnp.mean(xs * xs, axis=-1, keepdims=True) + EPS)

        def group_step(gi, w):
            def run():
                cols = slice(gi * POOL_GROUP_DIM, (gi + 1) * POOL_GROUP_DIM)
                gain, shift, scale = gm_ref[:, cols], mod_ref[3:4, cols], mod_ref[4:5, cols]

                def part(name, ref):
                    return ((ref[:, cols] * inv_rms[name]) * gain) * (1.0 + scale) + shift

                h = part("body", body)
                run_sum = jnp.concatenate([part("head", head) * head_on, h, part("tail", tail) * tail_on], axis=0)
                span = 1
                while span < w:
                    n = run_sum.shape[0]
                    run_sum = run_sum[:n - span] + run_sum[span:]
                    span *= 2
                start = POOL_HALO - w // 2
                seg = run_sum[start:start + rows]
                t_head = i * tm + r0 + lax.broadcasted_iota(jnp.int32, (POOL_HALO, 1), 0)
                t_tail = t_head + (rows - POOL_HALO)
                pooled = jnp.concatenate([seg[:POOL_HALO] / count(t_head, w),
                                          seg[POOL_HALO:rows - POOL_HALO] * (1.0 / w),
                                          seg[rows - POOL_HALO:] / count(t_tail, w)], axis=0)
                y = jnp.dot((pooled - h).astype(BF16), pw_ref[gi], preferred_element_type=F32)
                y = (y + pb_ref[:, cols]) * ps_ref[:, cols]
                x1_buf[k % 2, :, cols] = body[:, cols] + mod_ref[5:6, cols] * y
            return run

        return [stats] + [group_step(gi, w) for gi, w in enumerate(POOL_WINDOWS)]

    _swiglu_tile(prep, lambda k, r0, rows: x1_buf[k % 2], tm,
                 mod_ref, g_ref, w13_ref, w2_ref, o_ref, act_ref, h_buf, 6)


def _ffn_call(body, name, x, mixer_inputs, mixer_specs, mod, g, w13, w2, widx, tm):
    b, s, d = x.shape
    rows = FFN_ROW_BLOCK if tm % FFN_ROW_BLOCK == 0 else tm
    row = pl.BlockSpec((1, d), lambda bi, i: (0, 0))
    scratch = [pltpu.VMEM((tm, D_FF), BF16), pltpu.VMEM((2, rows, d), BF16)]
    if mixer_inputs:
        scratch.append(pltpu.VMEM((2, rows, d), F32))
    return pl.pallas_call(
        body,
        grid=(b, s // tm),
        in_specs=[pl.BlockSpec((None, tm, d), lambda bi, i: (bi, i, 0))] + list(mixer_specs) + [
            pl.BlockSpec((None, N_MOD, d), lambda bi, i: (bi, 0, 0)),
            row,
            _resident((None, d, 2 * D_FF), lambda bi, i: (widx, 0, 0)),
            _resident((None, D_FF, d), lambda bi, i: (widx, 0, 0)),
        ],
        out_specs=pl.BlockSpec((None, tm, d), lambda bi, i: (bi, i, 0)),
        out_shape=jax.ShapeDtypeStruct(x.shape, F32),
        scratch_shapes=scratch,
        compiler_params=_params("parallel", "parallel"),
        name=name,
    )(x, *mixer_inputs, mod, g.reshape(1, d), w13, w2)


def _ffn(x, mod, g, w13, w2, widx, tm):
    return _ffn_call(_ffn_kernel, "ffn", x, (), (), mod, g, w13, w2, widx, tm)


def _attn_mix_ffn(x, mix_a, mix_b, w_out, mod, g, w13, w2, widx, tm):
    half = mix_a.shape[-1]
    mix = pl.BlockSpec((None, tm, half), lambda bi, i: (bi, i, 0))
    specs = [mix, mix, _resident(w_out.shape, lambda bi, i: (0, 0))]
    return _ffn_call(_attn_mix_ffn_kernel, "attn_mix_ffn", x, (mix_a, mix_b, w_out), specs, mod, g, w13, w2, widx, tm)


def _pool_mix_ffn(x, g_mix, pool_w, pool_b, pool_scale, mod, g, w13, w2, widx, tm):
    b, s, d = x.shape
    per = tm // POOL_HALO
    last = s // POOL_HALO - 1
    row = pl.BlockSpec((1, d), lambda bi, i: (0, 0))
    specs = [
        pl.BlockSpec((None, POOL_HALO, d), lambda bi, i: (bi, jnp.maximum(i * per - 1, 0), 0)),
        pl.BlockSpec((None, POOL_HALO, d), lambda bi, i: (bi, jnp.minimum((i + 1) * per, last), 0)),
        row,
        pl.BlockSpec(pool_w.shape, lambda bi, i: (0, 0, 0)),
        row,
        row,
    ]
    inputs = (x, x, g_mix.reshape(1, d), pool_w, pool_b.reshape(1, d), pool_scale.reshape(1, d))
    return _ffn_call(functools.partial(_pool_mix_ffn_kernel, s_len=s), "pool_mix_ffn", x, inputs, specs,
                     mod, g, w13, w2, widx, tm)


def _proj_kernel(x_ref, mod_ref, g_ref, wt_ref, gain_ref, cos_ref, sin_ref,
                 qa_ref, ka_ref, va_ref, qb_ref, kb_ref, vb_ref, h_buf):
    tm = x_ref.shape[0]
    ts = PROJ_TOKEN_SUB if tm % PROJ_TOKEN_SUB == 0 else tm
    width = B_HEADS * HEAD_DIM

    def norm_step(j):
        h_buf[j % 2] = _norm_mod(x_ref[j * ts:(j + 1) * ts, :], g_ref[...], mod_ref[3:4, :],
                                 mod_ref[4:5, :]).astype(BF16)

    norm_step(0)
    for j in range(tm // ts):
        tok = slice(j * ts, (j + 1) * ts)
        h = h_buf[j % 2]
        cos = cos_ref[:, tok]
        sin = sin_ref[:, tok]

        def group_t(gi):
            return lax.dot_general(wt_ref[gi * width:(gi + 1) * width, :], h,
                                   (((1,), (1,)), ((), ())), preferred_element_type=F32)

        def head_norm(ut, gain):
            x3 = ut.reshape(B_HEADS, HEAD_DIM, ts)
            ms = jnp.mean(x3 * x3, axis=1, keepdims=True)
            return (x3 * lax.rsqrt(ms + EPS)) * gain[None]

        def rope(x3):
            x1 = x3[:, :HEAD_DIM // 2, :]
            x2 = x3[:, HEAD_DIM // 2:, :]
            return jnp.concatenate([x1 * cos - x2 * sin, x1 * sin + x2 * cos], axis=1)

        qa = rope(head_norm(group_t(0), gain_ref[0, :, tok])) * QK_SCALE
        qa_ref[:, :, tok] = qa.reshape(A_HEADS, A_VDIM, ts).astype(BF16)
        ka = rope(head_norm(group_t(1), gain_ref[1, :, tok])).reshape(A_HEADS, A_VDIM, ts)
        for hh in range(A_HEADS):
            ka_ref[hh, tok, :] = ka[hh].T.astype(BF16)
        if (j + 1) * ts < tm:
            norm_step(j + 1)
        qb = head_norm(group_t(3), gain_ref[2, :, tok]) * QK_SCALE
        qb_ref[:, :, tok] = qb.astype(BF16)
        kb = head_norm(group_t(4), gain_ref[3, :, tok]).reshape(B_HEADS // 2, 2 * HEAD_DIM, ts)
        for hp in range(B_HEADS // 2):
            kb_ref[hp, tok, :] = kb[hp].T.astype(BF16)
        va_ref[:, :A_VDIM, tok] = group_t(2).reshape(A_HEADS, A_VDIM, ts).astype(BF16)
        va_ref[:, A_VDIM:, tok] = jnp.ones((A_HEADS, DA_VROWS - A_VDIM, ts), BF16)
        vb_ref[:, :HEAD_DIM, tok] = group_t(5).reshape(B_HEADS, HEAD_DIM, ts).astype(BF16)
        vb_ref[:, HEAD_DIM:, tok] = jnp.ones((B_HEADS, NA_VROWS - HEAD_DIM, ts), BF16)


def _proj(x, mod, g, w_in_t, gains, cos_t, sin_t, tm):
    b, s, d = x.shape
    tok = lambda bi, i: (bi, 0, 0, i)
    seq = lambda bi, i: (bi, 0, i, 0)
    return pl.pallas_call(
        _proj_kernel,
        grid=(b, s // tm),
        in_specs=[
            pl.BlockSpec((None, tm, d), lambda bi, i: (bi, i, 0)),
            pl.BlockSpec((None, N_MOD, d), lambda bi, i: (bi, 0, 0)),
            pl.BlockSpec((1, d), lambda bi, i: (0, 0)),
            _resident(w_in_t.shape, lambda bi, i: (0, 0)),
            pl.BlockSpec((4, HEAD_DIM, tm), lambda bi, i: (0, 0, 0)),
            pl.BlockSpec((HEAD_DIM // 2, tm), lambda bi, i: (0, i)),
            pl.BlockSpec((HEAD_DIM // 2, tm), lambda bi, i: (0, i)),
        ],
        out_specs=[
            pl.BlockSpec((None, A_HEADS, A_VDIM, tm), tok),
            pl.BlockSpec((None, A_HEADS, tm, A_VDIM), seq),
            pl.BlockSpec((None, A_HEADS, DA_VROWS, tm), tok),
            pl.BlockSpec((None, B_HEADS, HEAD_DIM, tm), tok),
            pl.BlockSpec((None, B_HEADS // 2, tm, 2 * HEAD_DIM), seq),
            pl.BlockSpec((None, B_HEADS, NA_VROWS, tm), tok),
        ],
        out_shape=[
            jax.ShapeDtypeStruct((b, A_HEADS, A_VDIM, s), BF16),
            jax.ShapeDtypeStruct((b, A_HEADS, s, A_VDIM), BF16),
            jax.ShapeDtypeStruct((b, A_HEADS, DA_VROWS, s), BF16),
            jax.ShapeDtypeStruct((b, B_HEADS, HEAD_DIM, s), BF16),
            jax.ShapeDtypeStruct((b, B_HEADS // 2, s, 2 * HEAD_DIM), BF16),
            jax.ShapeDtypeStruct((b, B_HEADS, NA_VROWS, s), BF16),
        ],
        scratch_shapes=[pltpu.VMEM((2, PROJ_TOKEN_SUB if tm % PROJ_TOKEN_SUB == 0 else tm, d), BF16)],
        compiler_params=_params("parallel", "parallel"),
        name="qkv_proj",
    )(x, mod, g.reshape(1, d), w_in_t, gains, cos_t, sin_t)


def _dattn_kernel(q_ref, k_ref, v_ref, lam_ref, gn_ref, sg_ref, w13f_ref, w2f_ref, o_ref, w13b_ref, w2b_ref,
                  acc_ref, *, kblk, lambda_init):
    s_len = k_ref.shape[0]
    n_strips = q_ref.shape[1] // DA_QSUB
    lp = lam_ref[...]
    lam = (jnp.exp(jnp.sum(lp[0:1] * lp[1:2], axis=-1, keepdims=True))
           - jnp.exp(jnp.sum(lp[2:3] * lp[3:4], axis=-1, keepdims=True)) + lambda_init)
    gmax = jnp.max(jnp.abs(gn_ref[...]), axis=-1, keepdims=True)
    shift = (DA_BOUND_MARGIN * HEAD_DIM * QK_SCALE) * gmax[0:1] * gmax[1:2]
    z = jnp.zeros((HEAD_DIM, DA_QSUB), BF16)

    def strip_rhs(sb):
        qt = q_ref[:, sb * DA_QSUB:(sb + 1) * DA_QSUB]
        return jnp.concatenate([jnp.concatenate([qt[:HEAD_DIM], z], axis=0),
                                jnp.concatenate([z, qt[HEAD_DIM:]], axis=0)], axis=1)

    def finish(sb, num, den):
        o = num / den
        ot = o[:, :DA_QSUB] - lam * o[:, DA_QSUB:]
        ms = jnp.mean(ot * ot, axis=0, keepdims=True)
        y = ((ot * lax.rsqrt(ms + EPS)) * sg_ref[...]) * (1.0 - lambda_init)
        o_ref[sb * DA_QSUB:(sb + 1) * DA_QSUB, :] = y.T.astype(BF16)

    min_sum = None
    for sb in range(n_strips):
        if sb == n_strips // 2:
            w13b_ref[...] = w13f_ref[...].astype(BF16)
            w2b_ref[...] = w2f_ref[...].astype(BF16)
        rhs = strip_rhs(sb)
        num = None
        den8 = None
        for k0 in range(0, s_len, kblk):
            st = jnp.dot(k_ref[k0:k0 + kblk, :], rhs, preferred_element_type=F32)
            p = jnp.exp(st - shift)
            part = jnp.sum(p.reshape(kblk // 8, 8, 2 * DA_QSUB), axis=0)
            pv = jnp.dot(v_ref[:A_VDIM, k0:k0 + kblk], p.astype(BF16), preferred_element_type=F32)
            num = pv if num is None else num + pv
            den8 = part if den8 is None else den8 + part
        den = jnp.sum(den8, axis=0, keepdims=True)
        finish(sb, num, den)
        min_sum = den if min_sum is None else jnp.minimum(min_sum, den)

    sums_ok = jnp.min(min_sum) >= DA_MIN_SUM

    @pl.when(jnp.logical_not(sums_ok))
    def _():
        for sb in range(n_strips):
            rhs = strip_rhs(sb)
            acc_ref[...] = jnp.zeros_like(acc_ref)

            def safe_body(t, m):
                k0 = pl.multiple_of(t * DA_SAFE_KBLK, DA_SAFE_KBLK)
                st = jnp.dot(k_ref[pl.ds(k0, DA_SAFE_KBLK), :], rhs, preferred_element_type=F32)
                m_new = jnp.maximum(m, jnp.max(st, axis=0, keepdims=True))
                pv = jnp.dot(v_ref[:, pl.ds(k0, DA_SAFE_KBLK)], jnp.exp(st - m_new).astype(BF16),
                             preferred_element_type=F32)
                acc_ref[...] = acc_ref[...] * jnp.exp(m - m_new) + pv
                return m_new

            lax.fori_loop(0, s_len // DA_SAFE_KBLK, safe_body, jnp.full((1, 2 * DA_QSUB), NEG_INF, F32))
            finish(sb, acc_ref[:A_VDIM, :], acc_ref[A_VDIM:A_VDIM + 1, :])


def _cast_rows(total, steps):
    return min(r for r in range(BF16_SUBLANES, total + 1, BF16_SUBLANES) if total % r == 0 and total // r <= steps)


def _dattn(qa_t, ka, va_t, a_lambda, a_qk_gain, a_subln, lambda_init, qblk, ff_w13, ff_w2):
    b, nh, _, s = qa_t.shape
    assert qblk % DA_QSUB == 0
    nq = s // qblk
    depth, halves, d, f2 = ff_w13.shape
    n_sets = depth * halves - 1
    per_set = (b * nh * nq) // n_sets
    assert per_set >= 1
    rows13, rows2 = _cast_rows(d, per_set), _cast_rows(f2 // 2, per_set)

    def cast_block(rows_total, rows):
        nblk = rows_total // rows

        def locate(bi, h, i):
            blk = jnp.minimum((bi * nh + h) * nq + i, n_sets * nblk - 1)
            return blk // nblk, blk % nblk

        def src(bi, h, i):
            st, rb = locate(bi, h, i)
            return ((st + 1) // halves, (st + 1) % halves, rb, 0)

        def dst(bi, h, i):
            st, rb = locate(bi, h, i)
            return (st, rb, 0)

        return src, dst

    src13, dst13 = cast_block(d, rows13)
    src2, dst2 = cast_block(f2 // 2, rows2)
    return pl.pallas_call(
        functools.partial(_dattn_kernel, kblk=_tile(s, DA_KBLK), lambda_init=lambda_init),
        grid=(b, nh, nq),
        in_specs=[
            pl.BlockSpec((None, None, A_VDIM, qblk), lambda bi, h, i: (bi, h, 0, i)),
            pl.BlockSpec((None, None, s, A_VDIM), lambda bi, h, i: (bi, h, 0, 0)),
            pl.BlockSpec((None, None, DA_VROWS, s), lambda bi, h, i: (bi, h, 0, 0)),
            pl.BlockSpec((4, HEAD_DIM), lambda bi, h, i: (0, 0)),
            pl.BlockSpec((2, HEAD_DIM), lambda bi, h, i: (0, 0)),
            pl.BlockSpec((A_VDIM, DA_QSUB), lambda bi, h, i: (0, 0)),
            pl.BlockSpec((None, None, rows13, f2), src13),
            pl.BlockSpec((None, None, rows2, d), src2),
        ],
        out_specs=[
            pl.BlockSpec((None, qblk, A_VDIM), lambda bi, h, i: (bi, i, h)),
            pl.BlockSpec((None, rows13, f2), dst13),
            pl.BlockSpec((None, rows2, d), dst2),
        ],
        out_shape=[
            jax.ShapeDtypeStruct((b, s, nh * A_VDIM), BF16),
            jax.ShapeDtypeStruct((n_sets, d, f2), BF16),
            jax.ShapeDtypeStruct((n_sets, f2 // 2, d), BF16),
        ],
        scratch_shapes=[pltpu.VMEM((DA_VROWS, 2 * DA_QSUB), F32)],
        compiler_params=_params("arbitrary", "arbitrary", "arbitrary"),
        name="diff_attn",
    )(qa_t, ka, va_t, a_lambda, a_qk_gain, jnp.broadcast_to(a_subln[:, None], (A_VDIM, DA_QSUB)), ff_w13, ff_w2)


def _toeplitz_selector():
    kc = np.arange(GRID_W)[:, None]
    qc = np.arange(GRID_W)[None, :]
    cs = np.clip(qc - NA_COLS // 2, 0, GRID_W - NA_COLS)
    col_ok = (kc >= cs) & (kc < cs + NA_COLS)
    dc = np.clip(kc - qc + NA_COLS - 1, 0, 2 * NA_COLS - 2)
    sel = (np.arange(2 * NA_COLS)[:, None, None] == dc[None]) & col_ok[None]
    mask = np.where(col_ok, 0.0, NEG_INF)
    return (sel.reshape(2 * NA_COLS, GRID_W * GRID_W).astype(np.float32),
            mask.reshape(1, GRID_W * GRID_W).astype(np.float32))


def _rpb_expand_kernel(r_ref, gn_ref, sel_ref, mask_ref, o_ref, *, ndr, ndc):
    r = r_ref[...]
    nh = r.shape[0] // NA_DR_PAD
    rows = lax.broadcasted_iota(jnp.int32, r.shape, 0)
    cols = lax.broadcasted_iota(jnp.int32, r.shape, 1)
    real = ((rows & (NA_DR_PAD - 1)) < ndr) & (cols < ndc)
    row_max = jnp.max(jnp.where(real, r, NEG_INF), axis=1, keepdims=True)
    head_max = jnp.max(row_max.reshape(nh, NA_DR_PAD, 1), axis=1, keepdims=True)
    gmax = jnp.max(jnp.abs(gn_ref[...]), axis=-1, keepdims=True)
    bound = (DA_BOUND_MARGIN * HEAD_DIM * QK_SCALE) * gmax[0:1] * gmax[1:2]
    shift = jnp.broadcast_to(head_max + bound[None], (nh, NA_DR_PAD, 1)).reshape(r.shape[0], 1)
    o_ref[...] = (jnp.dot(r, sel_ref[...], preferred_element_type=F32, precision=lax.Precision.HIGHEST)
                  + mask_ref[...]) - shift


def _na_bias_table(rpb, qk_gain):
    nh, ndr, ndc = rpb.shape
    sel, mask = _toeplitz_selector()
    r = jnp.pad(rpb, ((0, 0), (0, NA_DR_PAD - ndr), (0, sel.shape[0] - ndc))).reshape(nh * NA_DR_PAD, sel.shape[0])
    flat = pl.pallas_call(
        functools.partial(_rpb_expand_kernel, ndr=ndr, ndc=ndc),
        out_shape=jax.ShapeDtypeStruct((nh * NA_DR_PAD, GRID_W * GRID_W), F32),
        name="rpb_expand",
    )(r, qk_gain, sel, mask)
    tile = flat.reshape(nh, NA_DR_PAD, GRID_W, GRID_W)
    neg = jnp.full((nh, GRID_W, GRID_W), NEG_INF, F32)
    half = NA_ROWS // 2
    kinds = []
    for kind in range(3):
        win_rows = []
        for i in range(NA_WIN_ROWS):
            blocks = []
            for j in range(NA_BLOCK_ROWS):
                if kind == 0:
                    dr, ok = i - j + NA_ROWS - 1, i < NA_ROWS
                elif kind == 1:
                    dr, ok = i - j + half - 1, j <= i < j + NA_ROWS
                else:
                    dr, ok = i - j - 1, i >= NA_WIN_ROWS - NA_ROWS
                blocks.append(tile[:, dr] if ok else neg)
            win_rows.append(jnp.concatenate(blocks, axis=-1))
        kinds.append(jnp.concatenate(win_rows, axis=1))
    return jnp.stack(kinds)


def _na_kernel(q_ref, k_ref, v_ref, bias_ref, o_ref, *, nblk):
    qn = NA_BLOCK_ROWS * GRID_W
    win = NA_WIN_ROWS * GRID_W
    nh = q_ref.shape[0]
    per_step = q_ref.shape[2] // qn
    z = jnp.zeros((HEAD_DIM, qn), BF16)

    def block(j):
        rb = pl.program_id(1) * per_step + j
        t0 = pl.multiple_of(jnp.clip(rb - 1, 0, nblk - 3) * qn, qn)
        kind = jnp.where(rb == 0, 0, jnp.where(rb == nblk - 1, 2, 1))

        def logits(h):
            qt = q_ref[h, :, j * qn:(j + 1) * qn]
            rhs = jnp.concatenate([qt, z] if h % 2 == 0 else [z, qt], axis=0)
            kwin = k_ref[h // 2, pl.ds(t0, win), :]
            return jnp.dot(kwin, rhs, preferred_element_type=F32) + bias_ref[kind, h]

        def weighted(h, p):
            return jnp.dot(v_ref[h, :, pl.ds(t0, win)], p.astype(BF16), preferred_element_type=F32)

        return logits, weighted

    def store(j, pvs):
        outs = [pv[:HEAD_DIM] / pv[HEAD_DIM:HEAD_DIM + 1] for pv in pvs]
        o_ref[j * qn:(j + 1) * qn, :] = jnp.concatenate(outs, axis=0).T.astype(BF16)

    sums = []
    for j in range(per_step):
        logits, weighted = block(j)
        sts = [logits(h) for h in range(nh)]
        pvs = [weighted(h, jnp.exp(st)) for h, st in enumerate(sts)]
        store(j, pvs)
        sums += [pv[HEAD_DIM:HEAD_DIM + 1] for pv in pvs]

    sums_ok = jnp.min(functools.reduce(jnp.minimum, sums)) >= NA_MIN_SUM

    @pl.when(jnp.logical_not(sums_ok))
    def _():
        for j in range(per_step):
            logits, weighted = block(j)
            safe = []
            for h in range(nh):
                st = logits(h)
                safe.append(weighted(h, jnp.exp(st - jnp.max(st, axis=0, keepdims=True))))
            store(j, safe)


def _na(qb_t, kb, vb_t, bias_tab):
    b, nh, _, s = qb_t.shape
    qn = NA_BLOCK_ROWS * GRID_W
    nblk = s // qn
    assert nblk >= 3
    per_step = NA_STEP_BLOCKS if nblk % NA_STEP_BLOCKS == 0 else 1
    qn *= per_step
    return pl.pallas_call(
        functools.partial(_na_kernel, nblk=nblk),
        grid=(b, nblk // per_step),
        in_specs=[
            pl.BlockSpec((None, nh, HEAD_DIM, qn), lambda bi, rb: (bi, 0, 0, rb)),
            _resident((None, nh // 2, s, 2 * HEAD_DIM), lambda bi, rb: (bi, 0, 0, 0)),
            _resident((None, nh, NA_VROWS, s), lambda bi, rb: (bi, 0, 0, 0)),
            _resident(bias_tab.shape, lambda bi, rb: (0, 0, 0, 0)),
        ],
        out_specs=pl.BlockSpec((None, qn, nh * HEAD_DIM), lambda bi, rb: (bi, rb, 0)),
        out_shape=jax.ShapeDtypeStruct((b, s, nh * HEAD_DIM), BF16),
        compiler_params=_params("parallel", "arbitrary"),
        name="nbr_attn",
    )(qb_t, kb, vb_t, bias_tab)


def _tile(s, want):
    return want if s % want == 0 else s


def kernel(x, c, ada_w, ada_b, norm_g, ff_w13, ff_w2, w_in, w_out, a_qk_norm, a_lambda, a_subln,
           b_qk_norm, b_rpb, pool_w, pool_b, pool_scale):
    b, s, d = x.shape
    depth = ada_w.shape[0]
    tm = _tile(s, TOKEN_TILE)
    ffn_tm = _tile(s, FFN_TOKEN_TILE)
    mod = _adaln(c, ada_w, ada_b)

    pos = jnp.arange(s, dtype=F32)
    inv = ROPE_THETA ** (-jnp.arange(0, HEAD_DIM, 2, dtype=F32) / HEAD_DIM)
    ang = inv[:, None] * pos[None, :]
    cos_t, sin_t = jnp.cos(ang), jnp.sin(ang)

    first = (ff_w13[0, :1].astype(BF16), ff_w2[0, :1].astype(BF16))
    rest = None

    def ffn_weights(layer, half):
        idx = 2 * layer + half
        return (*first, 0) if idx == 0 else (*rest, idx - 1)

    for layer in range(depth):
        m = mod[layer]
        x = _ffn(x, m, norm_g[layer, 0], *ffn_weights(layer, 0), ffn_tm)
        if layer % 2 == 0:
            e = layer // 2
            lambda_init = 0.8 - 0.6 * math.exp(-0.3 * layer)
            gains = jnp.stack([a_qk_norm[e, 0], a_qk_norm[e, 1], b_qk_norm[e, 0], b_qk_norm[e, 1]])
            gains = jnp.broadcast_to(gains[:, :, None], (4, HEAD_DIM, tm))
            qa_t, ka, va_t, qb_t, kb, vb_t = _proj(x, m, norm_g[layer, 1], w_in[e].T.astype(BF16),
                                                   gains, cos_t, sin_t, tm)
            mix_a, w13_rest, w2_rest = _dattn(qa_t, ka, va_t, a_lambda[e], a_qk_norm[e], a_subln[e], lambda_init,
                                              _tile(s, DA_QBLK), ff_w13, ff_w2)
            if rest is None:
                rest = (w13_rest, w2_rest)
            mix_b = _na(qb_t, kb, vb_t, _na_bias_table(b_rpb[e], b_qk_norm[e]))
            x = _attn_mix_ffn(x, mix_a, mix_b, w_out[e].astype(BF16), m, norm_g[layer, 2],
                              *ffn_weights(layer, 1), ffn_tm)
        else:
            o = layer // 2
            x = _pool_mix_ffn(x, norm_g[layer, 1], pool_w[o].astype(BF16), pool_b[o], pool_scale[o],
                              m, norm_g[layer, 2], *ffn_weights(layer, 1), ffn_tm)
    return x
```

```python
import functools
import math

import numpy as np
import jax
import jax.numpy as jnp
from jax import lax
from jax.experimental import pallas as pl
from jax.experimental.pallas import tpu as pltpu

F32 = jnp.float32
BF16 = jnp.bfloat16

D_MODEL = 1024
HEAD_DIM = 64
A_HEADS = 4
A_VDIM = 2 * HEAD_DIM
B_HEADS = 8
D_FF = 2816
GRID_W = 64
NA_ROWS = 8
NA_COLS = 16
POOL_WINDOWS = (2, 4, 8, 16)
POOL_GROUP_DIM = D_MODEL // len(POOL_WINDOWS)
ROPE_THETA = 10000.0
EPS = 1e-6
N_MOD = 9
NEG_INF = -1e30
QK_SCALE = HEAD_DIM ** -0.5

V7X_VMEM_LIMIT_BYTES = 56 * 1024 * 1024
BF16_SUBLANES = 16
ADALN_K_TILE = 256
TOKEN_TILE = 1024
PROJ_TOKEN_SUB = 256
FFN_TOKEN_TILE = 1024
FFN_ROW_BLOCK = 256
FF_CHUNK = 256
DA_QBLK = 1024
DA_QSUB = 256
DA_KBLK = 8192
DA_BOUND_MARGIN = 1.02
DA_MIN_SUM = 1e-18
DA_SAFE_KBLK = 256
DA_VROWS = A_VDIM + BF16_SUBLANES
NA_BLOCK_ROWS = 4
NA_WIN_ROWS = 3 * NA_BLOCK_ROWS
NA_STEP_BLOCKS = 4
NA_VROWS = HEAD_DIM + BF16_SUBLANES
NA_DR_PAD = 16
NA_MIN_SUM = 1e-18
POOL_HALO = 8


def _params(*sem):
    return pltpu.CompilerParams(dimension_semantics=sem, vmem_limit_bytes=V7X_VMEM_LIMIT_BYTES)


def _resident(shape, index_map):
    return pl.BlockSpec(shape, index_map, pipeline_mode=pl.Buffered(1))


def _norm_mod(x, g, shift, scale):
    ms = jnp.mean(x * x, axis=-1, keepdims=True)
    y = (x * lax.rsqrt(ms + EPS)) * g
    return y * (1.0 + scale) + shift


def _adaln_kernel(c_ref, w_ref, b_ref, o_ref):
    tk = w_ref.shape[0]
    k = pl.program_id(1)
    c = c_ref[:, pl.ds(pl.multiple_of(k * tk, tk), tk)]
    cond = c / (1.0 + jnp.exp(-c))

    @pl.when(k == 0)
    def _():
        o_ref[...] = jnp.broadcast_to(b_ref[...], o_ref.shape)

    o_ref[...] += jnp.dot(cond.astype(BF16), w_ref[...].astype(BF16), preferred_element_type=F32)


def _adaln(c, ada_w, ada_b):
    depth, d, n = ada_w.shape
    b = c.shape[0]
    bp = -(-b // 8) * 8
    cp = jnp.pad(c, ((0, bp - b), (0, 0)))
    tk = ADALN_K_TILE
    out = pl.pallas_call(
        _adaln_kernel,
        grid=(depth, d // tk),
        in_specs=[
            pl.BlockSpec((bp, d), lambda l, k: (0, 0)),
            pl.BlockSpec((None, tk, n), lambda l, k: (l, k, 0)),
            pl.BlockSpec((None, 1, n), lambda l, k: (l, 0, 0)),
        ],
        out_specs=pl.BlockSpec((None, bp, n), lambda l, k: (l, 0, 0)),
        out_shape=jax.ShapeDtypeStruct((depth, bp, n), F32),
        compiler_params=_params("arbitrary", "arbitrary"),
        name="adaln",
    )(cp, ada_w, ada_b.reshape(depth, 1, n))
    return out[:, :b].reshape(depth, b, N_MOD, d)


def _swiglu_tile(prep, x_rows, tm, mod_ref, g_ref, w13_ref, w2_ref, o_ref, act_ref, h_buf, row0):
    rows = FFN_ROW_BLOCK if tm % FFN_ROW_BLOCK == 0 else tm
    n_blocks = tm // rows
    n_chunks = D_FF // FF_CHUNK
    gate = 0.5 * mod_ref[row0 + 2:row0 + 3, :]

    def norm_step(k, r0):
        def run():
            h_buf[k % 2] = _norm_mod(x_rows(k, r0, rows), g_ref[...], mod_ref[row0:row0 + 1, :],
                                     mod_ref[row0 + 1:row0 + 2, :]).astype(BF16)
        return run

    steps = [list(prep(k, k * rows, rows)) + [norm_step(k, k * rows)] for k in range(n_blocks)]
    for step in steps[0]:
        step()
    for k in range(n_blocks):
        r0 = k * rows
        side = steps[k + 1] if k + 1 < n_blocks else []
        after = {((j + 1) * n_chunks) // (len(side) + 1) - 1: step for j, step in enumerate(side)}
        assert len(after) == len(side)
        h = h_buf[k % 2]
        for ci in range(n_chunks):
            lo = ci * FF_CHUNK
            a = jnp.dot(h, w13_ref[:, lo:lo + FF_CHUNK], preferred_element_type=F32)
            b = jnp.dot(h, w13_ref[:, D_FF + lo:D_FF + lo + FF_CHUNK], preferred_element_type=F32)
            act_ref[r0:r0 + rows, lo:lo + FF_CHUNK] = ((a / (1.0 + jnp.exp(-a))) * b).astype(BF16)
            if ci in after:
                after[ci]()
        y = jnp.dot(act_ref[r0:r0 + rows, :], w2_ref[...], preferred_element_type=F32)
        o_ref[r0:r0 + rows, :] = x_rows(k, r0, rows) + gate * y


def _ffn_kernel(x_ref, mod_ref, g_ref, w13_ref, w2_ref, o_ref, act_ref, h_buf):
    _swiglu_tile(lambda k, r0, rows: (), lambda k, r0, rows: x_ref[r0:r0 + rows, :], x_ref.shape[0],
                 mod_ref, g_ref, w13_ref, w2_ref, o_ref, act_ref, h_buf, 0)


def _attn_mix_ffn_kernel(x_ref, ma_ref, mb_ref, wo_ref, mod_ref, g_ref, w13_ref, w2_ref, o_ref,
                         act_ref, h_buf, x1_buf):
    half = ma_ref.shape[1]

    def prep(k, r0, rows):
        def mix():
            y = (jnp.dot(ma_ref[r0:r0 + rows, :], wo_ref[:half, :], preferred_element_type=F32)
                 + jnp.dot(mb_ref[r0:r0 + rows, :], wo_ref[half:, :], preferred_element_type=F32))
            x1_buf[k % 2] = x_ref[r0:r0 + rows, :] + mod_ref[5:6, :] * y
        return [mix]

    _swiglu_tile(prep, lambda k, r0, rows: x1_buf[k % 2], x_ref.shape[0],
                 mod_ref, g_ref, w13_ref, w2_ref, o_ref, act_ref, h_buf, 6)


def _pool_mix_ffn_kernel(x_ref, xp_ref, xn_ref, gm_ref, pw_ref, pb_ref, ps_ref, mod_ref, g_ref, w13_ref, w2_ref,
                         o_ref, act_ref, h_buf, x1_buf, *, s_len):
    tm = x_ref.shape[0]
    i = pl.program_id(1)

    def count(t, w):
        return (jnp.minimum(t + w // 2, s_len) - jnp.maximum(t - w // 2, 0)).astype(F32)

    def prep(k, r0, rows):
        if r0 == 0:
            head, head_on = xp_ref, (i > 0).astype(F32)
        else:
            head, head_on = x_ref.at[r0 - POOL_HALO:r0, :], 1.0
        if r0 + rows == tm:
            tail, tail_on = xn_ref, (i < pl.num_programs(1) - 1).astype(F32)
        else:
            tail, tail_on = x_ref.at[r0 + rows:r0 + rows + POOL_HALO, :], 1.0
        body = x_ref.at[r0:r0 + rows, :]
        inv_rms = {}

        def stats():
            for name, ref in (("head", head), ("body", body), ("tail", tail)):
                xs = ref[...]
                inv_rms[name] = lax.rsqrt(jnp.mean(xs * xs, axis=-1, keepdims=True) + EPS)

        def group_step(gi, w):
            def run():
                cols = slice(gi * POOL_GROUP_DIM, (gi + 1) * POOL_GROUP_DIM)
                gain, shift, scale = gm_ref[:, cols], mod_ref[3:4, cols], mod_ref[4:5, cols]

                def part(name, ref):
                    return ((ref[:, cols] * inv_rms[name]) * gain) * (1.0 + scale) + shift

                h = part("body", body)
                run_sum = jnp.concatenate([part("head", head) * head_on, h, part("tail", tail) * tail_on], axis=0)
                span = 1
                while span < w:
                    n = run_sum.shape[0]
                    run_sum = run_sum[:n - span] + run_sum[span:]
                    span *= 2
                start = POOL_HALO - w // 2
                seg = run_sum[start:start + rows]
                t_head = i * tm + r0 + lax.broadcasted_iota(jnp.int32, (POOL_HALO, 1), 0)
                t_tail = t_head + (rows - POOL_HALO)
                pooled = jnp.concatenate([seg[:POOL_HALO] / count(t_head, w),
                                          seg[POOL_HALO:rows - POOL_HALO] * (1.0 / w),
                                          seg[rows - POOL_HALO:] / count(t_tail, w)], axis=0)
                y = jnp.dot((pooled - h).astype(BF16), pw_ref[gi], preferred_element_type=F32)
                y = (y + pb_ref[:, cols]) * ps_ref[:, cols]
                x1_buf[k % 2, :, cols] = body[:, cols] + mod_ref[5:6, cols] * y
            return run

        return [stats] + [group_step(gi, w) for gi, w in enumerate(POOL_WINDOWS)]

    _swiglu_tile(prep, lambda k, r0, rows: x1_buf[k % 2], tm,
                 mod_ref, g_ref, w13_ref, w2_ref, o_ref, act_ref, h_buf, 6)


def _ffn_call(body, name, x, mixer_inputs, mixer_specs, mod, g, w13, w2, widx, tm):
    b, s, d = x.shape
    rows = FFN_ROW_BLOCK if tm % FFN_ROW_BLOCK == 0 else tm
    row = pl.BlockSpec((1, d), lambda bi, i: (0, 0))
    scratch = [pltpu.VMEM((tm, D_FF), BF16), pltpu.VMEM((2, rows, d), BF16)]
    if mixer_inputs:
        scratch.append(pltpu.VMEM((2, rows, d), F32))
    return pl.pallas_call(
        body,
        grid=(b, s // tm),
        in_specs=[pl.BlockSpec((None, tm, d), lambda bi, i: (bi, i, 0))] + list(mixer_specs) + [
            pl.BlockSpec((None, N_MOD, d), lambda bi, i: (bi, 0, 0)),
            row,
            _resident((None, d, 2 * D_FF), lambda bi, i: (widx, 0, 0)),
            _resident((None, D_FF, d), lambda bi, i: (widx, 0, 0)),
        ],
        out_specs=pl.BlockSpec((None, tm, d), lambda bi, i: (bi, i, 0)),
        out_shape=jax.ShapeDtypeStruct(x.shape, F32),
        scratch_shapes=scratch,
        compiler_params=_params("parallel", "parallel"),
        name=name,
    )(x, *mixer_inputs, mod, g.reshape(1, d), w13, w2)


def _ffn(x, mod, g, w13, w2, widx, tm):
    return _ffn_call(_ffn_kernel, "ffn", x, (), (), mod, g, w13, w2, widx, tm)


def _attn_mix_ffn(x, mix_a, mix_b, w_out, mod, g, w13, w2, widx, tm):
    half = mix_a.shape[-1]
    mix = pl.BlockSpec((None, tm, half), lambda bi, i: (bi, i, 0))
    specs = [mix, mix, _resident(w_out.shape, lambda bi, i: (0, 0))]
    return _ffn_call(_attn_mix_ffn_kernel, "attn_mix_ffn", x, (mix_a, mix_b, w_out), specs, mod, g, w13, w2, widx, tm)


def _pool_mix_ffn(x, g_mix, pool_w, pool_b, pool_scale, mod, g, w13, w2, widx, tm):
    b, s, d = x.shape
    per = tm // POOL_HALO
    last = s // POOL_HALO - 1
    row = pl.BlockSpec((1, d), lambda bi, i: (0, 0))
    specs = [
        pl.BlockSpec((None, POOL_HALO, d), lambda bi, i: (bi, jnp.maximum(i * per - 1, 0), 0)),
        pl.BlockSpec((None, POOL_HALO, d), lambda bi, i: (bi, jnp.minimum((i + 1) * per, last), 0)),
        row,
        pl.BlockSpec(pool_w.shape, lambda bi, i: (0, 0, 0)),
        row,
        row,
    ]
    inputs = (x, x, g_mix.reshape(1, d), pool_w, pool_b.reshape(1, d), pool_scale.reshape(1, d))
    return _ffn_call(functools.partial(_pool_mix_ffn_kernel, s_len=s), "pool_mix_ffn", x, inputs, specs,
                     mod, g, w13, w2, widx, tm)


def _proj_kernel(x_ref, mod_ref, g_ref, wt_ref, gain_ref, cos_ref, sin_ref,
                 qa_ref, ka_ref, va_ref, qb_ref, kb_ref, vb_ref, h_buf):
    tm = x_ref.shape[0]
    ts = PROJ_TOKEN_SUB if tm % PROJ_TOKEN_SUB == 0 else tm
    width = B_HEADS * HEAD_DIM

    def norm_step(j):
        h_buf[j % 2] = _norm_mod(x_ref[j * ts:(j + 1) * ts, :], g_ref[...], mod_ref[3:4, :],
                                 mod_ref[4:5, :]).astype(BF16)

    norm_step(0)
    for j in range(tm // ts):
        tok = slice(j * ts, (j + 1) * ts)
        h = h_buf[j % 2]
        cos = cos_ref[:, tok]
        sin = sin_ref[:, tok]

        def group_t(gi):
            return lax.dot_general(wt_ref[gi * width:(gi + 1) * width, :], h,
                                   (((1,), (1,)), ((), ())), preferred_element_type=F32)

        def head_norm(ut, gain):
            x3 = ut.reshape(B_HEADS, HEAD_DIM, ts)
            ms = jnp.mean(x3 * x3, axis=1, keepdims=True)
            return (x3 * lax.rsqrt(ms + EPS)) * gain[None]

        def rope(x3):
            x1 = x3[:, :HEAD_DIM // 2, :]
            x2 = x3[:, HEAD_DIM // 2:, :]
            return jnp.concatenate([x1 * cos - x2 * sin, x1 * sin + x2 * cos], axis=1)

        qa = rope(head_norm(group_t(0), gain_ref[0, :, tok])) * QK_SCALE
        qa_ref[:, :, tok] = qa.reshape(A_HEADS, A_VDIM, ts).astype(BF16)
        ka = rope(head_norm(group_t(1), gain_ref[1, :, tok])).reshape(A_HEADS, A_VDIM, ts)
        for hh in range(A_HEADS):
            ka_ref[hh, tok, :] = ka[hh].T.astype(BF16)
        if (j + 1) * ts < tm:
            norm_step(j + 1)
        qb = head_norm(group_t(3), gain_ref[2, :, tok]) * QK_SCALE
        qb_ref[:, :, tok] = qb.astype(BF16)
        kb = head_norm(group_t(4), gain_ref[3, :, tok]).reshape(B_HEADS // 2, 2 * HEAD_DIM, ts)
        for hp in range(B_HEADS // 2):
            kb_ref[hp, tok, :] = kb[hp].T.astype(BF16)
        va_ref[:, :A_VDIM, tok] = group_t(2).reshape(A_HEADS, A_VDIM, ts).astype(BF16)
        va_ref[:, A_VDIM:, tok] = jnp.ones((A_HEADS, DA_VROWS - A_VDIM, ts), BF16)
        vb_ref[:, :HEAD_DIM, tok] = group_t(5).reshape(B_HEADS, HEAD_DIM, ts).astype(BF16)
        vb_ref[:, HEAD_DIM:, tok] = jnp.ones((B_HEADS, NA_VROWS - HEAD_DIM, ts), BF16)


def _proj(x, mod, g, w_in_t, gains, cos_t, sin_t, tm):
    b, s, d = x.shape
    tok = lambda bi, i: (bi, 0, 0, i)
    seq = lambda bi, i: (bi, 0, i, 0)
    return pl.pallas_call(
        _proj_kernel,
        grid=(b, s // tm),
        in_specs=[
            pl.BlockSpec((None, tm, d), lambda bi, i: (bi, i, 0)),
            pl.BlockSpec((None, N_MOD, d), lambda bi, i: (bi, 0, 0)),
            pl.BlockSpec((1, d), lambda bi, i: (0, 0)),
            _resident(w_in_t.shape, lambda bi, i: (0, 0)),
            pl.BlockSpec((4, HEAD_DIM, tm), lambda bi, i: (0, 0, 0)),
            pl.BlockSpec((HEAD_DIM // 2, tm), lambda bi, i: (0, i)),
            pl.BlockSpec((HEAD_DIM // 2, tm), lambda bi, i: (0, i)),
        ],
        out_specs=[
            pl.BlockSpec((None, A_HEADS, A_VDIM, tm), tok),
            pl.BlockSpec((None, A_HEADS, tm, A_VDIM), seq),
            pl.BlockSpec((None, A_HEADS, DA_VROWS, tm), tok),
            pl.BlockSpec((None, B_HEADS, HEAD_DIM, tm), tok),
            pl.BlockSpec((None, B_HEADS // 2, tm, 2 * HEAD_DIM), seq),
            pl.BlockSpec((None, B_HEADS, NA_VROWS, tm), tok),
        ],
        out_shape=[
            jax.ShapeDtypeStruct((b, A_HEADS, A_VDIM, s), BF16),
            jax.ShapeDtypeStruct((b, A_HEADS, s, A_VDIM), BF16),
            jax.ShapeDtypeStruct((b, A_HEADS, DA_VROWS, s), BF16),
            jax.ShapeDtypeStruct((b, B_HEADS, HEAD_DIM, s), BF16),
            jax.ShapeDtypeStruct((b, B_HEADS // 2, s, 2 * HEAD_DIM), BF16),
            jax.ShapeDtypeStruct((b, B_HEADS, NA_VROWS, s), BF16),
        ],
        scratch_shapes=[pltpu.VMEM((2, PROJ_TOKEN_SUB if tm % PROJ_TOKEN_SUB == 0 else tm, d), BF16)],
        compiler_params=_params("parallel", "parallel"),
        name="qkv_proj",
    )(x, mod, g.reshape(1, d), w_in_t, gains, cos_t, sin_t)


def _dattn_kernel(q_ref, k_ref, v_ref, lam_ref, gn_ref, sg_ref, w13f_ref, w2f_ref, o_ref, w13b_ref, w2b_ref,
                  acc_ref, *, kblk, lambda_init):
    s_len = k_ref.shape[0]
    n_strips = q_ref.shape[1] // DA_QSUB
    lp = lam_ref[...]
    lam = (jnp.exp(jnp.sum(lp[0:1] * lp[1:2], axis=-1, keepdims=True))
           - jnp.exp(jnp.sum(lp[2:3] * lp[3:4], axis=-1, keepdims=True)) + lambda_init)
    gmax = jnp.max(jnp.abs(gn_ref[...]), axis=-1, keepdims=True)
    shift = (DA_BOUND_MARGIN * HEAD_DIM * QK_SCALE) * gmax[0:1] * gmax[1:2]
    z = jnp.zeros((HEAD_DIM, DA_QSUB), BF16)

    def strip_rhs(sb):
        qt = q_ref[:, sb * DA_QSUB:(sb + 1) * DA_QSUB]
        return jnp.concatenate([jnp.concatenate([qt[:HEAD_DIM], z], axis=0),
                                jnp.concatenate([z, qt[HEAD_DIM:]], axis=0)], axis=1)

    def finish(sb, num, den):
        o = num / den
        ot = o[:, :DA_QSUB] - lam * o[:, DA_QSUB:]
        ms = jnp.mean(ot * ot, axis=0, keepdims=True)
        y = ((ot * lax.rsqrt(ms + EPS)) * sg_ref[...]) * (1.0 - lambda_init)
        o_ref[sb * DA_QSUB:(sb + 1) * DA_QSUB, :] = y.T.astype(BF16)

    min_sum = None
    for sb in range(n_strips):
        if sb == n_strips // 2:
            w13b_ref[...] = w13f_ref[...].astype(BF16)
            w2b_ref[...] = w2f_ref[...].astype(BF16)
        rhs = strip_rhs(sb)
        num = None
        den8 = None
        for k0 in range(0, s_len, kblk):
            st = jnp.dot(k_ref[k0:k0 + kblk, :], rhs, preferred_element_type=F32)
            p = jnp.exp(st - shift)
            part = jnp.sum(p.reshape(kblk // 8, 8, 2 * DA_QSUB), axis=0)
            pv = jnp.dot(v_ref[:A_VDIM, k0:k0 + kblk], p.astype(BF16), preferred_element_type=F32)
            num = pv if num is None else num + pv
            den8 = part if den8 is None else den8 + part
        den = jnp.sum(den8, axis=0, keepdims=True)
        finish(sb, num, den)
        min_sum = den if min_sum is None else jnp.minimum(min_sum, den)

    sums_ok = jnp.min(min_sum) >= DA_MIN_SUM

    @pl.when(jnp.logical_not(sums_ok))
    def _():
        for sb in range(n_strips):
            rhs = strip_rhs(sb)
            acc_ref[...] = jnp.zeros_like(acc_ref)

            def safe_body(t, m):
                k0 = pl.multiple_of(t * DA_SAFE_KBLK, DA_SAFE_KBLK)
                st = jnp.dot(k_ref[pl.ds(k0, DA_SAFE_KBLK), :], rhs, preferred_element_type=F32)
                m_new = jnp.maximum(m, jnp.max(st, axis=0, keepdims=True))
                pv = jnp.dot(v_ref[:, pl.ds(k0, DA_SAFE_KBLK)], jnp.exp(st - m_new).astype(BF16),
                             preferred_element_type=F32)
                acc_ref[...] = acc_ref[...] * jnp.exp(m - m_new) + pv
                return m_new

            lax.fori_loop(0, s_len // DA_SAFE_KBLK, safe_body, jnp.full((1, 2 * DA_QSUB), NEG_INF, F32))
            finish(sb, acc_ref[:A_VDIM, :], acc_ref[A_VDIM:A_VDIM + 1, :])


def _cast_rows(total, steps):
    return min(r for r in range(BF16_SUBLANES, total + 1, BF16_SUBLANES) if total % r == 0 and total // r <= steps)


def _dattn(qa_t, ka, va_t, a_lambda, a_qk_gain, a_subln, lambda_init, qblk, ff_w13, ff_w2):
    b, nh, _, s = qa_t.shape
    assert qblk % DA_QSUB == 0
    nq = s // qblk
    depth, halves, d, f2 = ff_w13.shape
    n_sets = depth * halves - 1
    per_set = (b * nh * nq) // n_sets
    assert per_set >= 1
    rows13, rows2 = _cast_rows(d, per_set), _cast_rows(f2 // 2, per_set)

    def cast_block(rows_total, rows):
        nblk = rows_total // rows

        def locate(bi, h, i):
            blk = jnp.minimum((bi * nh + h) * nq + i, n_sets * nblk - 1)
            return blk // nblk, blk % nblk

        def src(bi, h, i):
            st, rb = locate(bi, h, i)
            return ((st + 1) // halves, (st + 1) % halves, rb, 0)

        def dst(bi, h, i):
            st, rb = locate(bi, h, i)
            return (st, rb, 0)

        return src, dst

    src13, dst13 = cast_block(d, rows13)
    src2, dst2 = cast_block(f2 // 2, rows2)
    return pl.pallas_call(
        functools.partial(_dattn_kernel, kblk=_tile(s, DA_KBLK), lambda_init=lambda_init),
        grid=(b, nh, nq),
        in_specs=[
            pl.BlockSpec((None, None, A_VDIM, qblk), lambda bi, h, i: (bi, h, 0, i)),
            pl.BlockSpec((None, None, s, A_VDIM), lambda bi, h, i: (bi, h, 0, 0)),
            pl.BlockSpec((None, None, DA_VROWS, s), lambda bi, h, i: (bi, h, 0, 0)),
            pl.BlockSpec((4, HEAD_DIM), lambda bi, h, i: (0, 0)),
            pl.BlockSpec((2, HEAD_DIM), lambda bi, h, i: (0, 0)),
            pl.BlockSpec((A_VDIM, DA_QSUB), lambda bi, h, i: (0, 0)),
            pl.BlockSpec((None, None, rows13, f2), src13),
            pl.BlockSpec((None, None, rows2, d), src2),
        ],
        out_specs=[
            pl.BlockSpec((None, qblk, A_VDIM), lambda bi, h, i: (bi, i, h)),
            pl.BlockSpec((None, rows13, f2), dst13),
            pl.BlockSpec((None, rows2, d), dst2),
        ],
        out_shape=[
            jax.ShapeDtypeStruct((b, s, nh * A_VDIM), BF16),
            jax.ShapeDtypeStruct((n_sets, d, f2), BF16),
            jax.ShapeDtypeStruct((n_sets, f2 // 2, d), BF16),
        ],
        scratch_shapes=[pltpu.VMEM((DA_VROWS, 2 * DA_QSUB), F32)],
        compiler_params=_params("arbitrary", "arbitrary", "arbitrary"),
        name="diff_attn",
    )(qa_t, ka, va_t, a_lambda, a_qk_gain, jnp.broadcast_to(a_subln[:, None], (A_VDIM, DA_QSUB)), ff_w13, ff_w2)


def _toeplitz_selector():
    kc = np.arange(GRID_W)[:, None]
    qc = np.arange(GRID_W)[None, :]
    cs = np.clip(qc - NA_COLS // 2, 0, GRID_W - NA_COLS)
    col_ok = (kc >= cs) & (kc < cs + NA_COLS)
    dc = np.clip(kc - qc + NA_COLS - 1, 0, 2 * NA_COLS - 2)
    sel = (np.arange(2 * NA_COLS)[:, None, None] == dc[None]) & col_ok[None]
    mask = np.where(col_ok, 0.0, NEG_INF)
    return (sel.reshape(2 * NA_COLS, GRID_W * GRID_W).astype(np.float32),
            mask.reshape(1, GRID_W * GRID_W).astype(np.float32))


def _rpb_expand_kernel(r_ref, gn_ref, sel_ref, mask_ref, o_ref, *, ndr, ndc):
    r = r_ref[...]
    nh = r.shape[0] // NA_DR_PAD
    rows = lax.broadcasted_iota(jnp.int32, r.shape, 0)
    cols = lax.broadcasted_iota(jnp.int32, r.shape, 1)
    real = ((rows & (NA_DR_PAD - 1)) < ndr) & (cols < ndc)
    row_max = jnp.max(jnp.where(real, r, NEG_INF), axis=1, keepdims=True)
    head_max = jnp.max(row_max.reshape(nh, NA_DR_PAD, 1), axis=1, keepdims=True)
    gmax = jnp.max(jnp.abs(gn_ref[...]), axis=-1, keepdims=True)
    bound = (DA_BOUND_MARGIN * HEAD_DIM * QK_SCALE) * gmax[0:1] * gmax[1:2]
    shift = jnp.broadcast_to(head_max + bound[None], (nh, NA_DR_PAD, 1)).reshape(r.shape[0], 1)
    o_ref[...] = (jnp.dot(r, sel_ref[...], preferred_element_type=F32, precision=lax.Precision.HIGHEST)
                  + mask_ref[...]) - shift


def _na_bias_table(rpb, qk_gain):
    nh, ndr, ndc = rpb.shape
    sel, mask = _toeplitz_selector()
    r = jnp.pad(rpb, ((0, 0), (0, NA_DR_PAD - ndr), (0, sel.shape[0] - ndc))).reshape(nh * NA_DR_PAD, sel.shape[0])
    flat = pl.pallas_call(
        functools.partial(_rpb_expand_kernel, ndr=ndr, ndc=ndc),
        out_shape=jax.ShapeDtypeStruct((nh * NA_DR_PAD, GRID_W * GRID_W), F32),
        name="rpb_expand",
    )(r, qk_gain, sel, mask)
    tile = flat.reshape(nh, NA_DR_PAD, GRID_W, GRID_W)
    neg = jnp.full((nh, GRID_W, GRID_W), NEG_INF, F32)
    half = NA_ROWS // 2
    kinds = []
    for kind in range(3):
        win_rows = []
        for i in range(NA_WIN_ROWS):
            blocks = []
            for j in range(NA_BLOCK_ROWS):
                if kind == 0:
                    dr, ok = i - j + NA_ROWS - 1, i < NA_ROWS
                elif kind == 1:
                    dr, ok = i - j + half - 1, j <= i < j + NA_ROWS
                else:
                    dr, ok = i - j - 1, i >= NA_WIN_ROWS - NA_ROWS
                blocks.append(tile[:, dr] if ok else neg)
            win_rows.append(jnp.concatenate(blocks, axis=-1))
        kinds.append(jnp.concatenate(win_rows, axis=1))
    return jnp.stack(kinds)


def _na_kernel(q_ref, k_ref, v_ref, bias_ref, o_ref, *, nblk):
    qn = NA_BLOCK_ROWS * GRID_W
    win = NA_WIN_ROWS * GRID_W
    nh = q_ref.shape[0]
    per_step = q_ref.shape[2] // qn
    z = jnp.zeros((HEAD_DIM, qn), BF16)

    def block(j):
        rb = pl.program_id(1) * per_step + j
        t0 = pl.multiple_of(jnp.clip(rb - 1, 0, nblk - 3) * qn, qn)
        kind = jnp.where(rb == 0, 0, jnp.where(rb == nblk - 1, 2, 1))

        def logits(h):
            qt = q_ref[h, :, j * qn:(j + 1) * qn]
            rhs = jnp.concatenate([qt, z] if h % 2 == 0 else [z, qt], axis=0)
            kwin = k_ref[h // 2, pl.ds(t0, win), :]
            return jnp.dot(kwin, rhs, preferred_element_type=F32) + bias_ref[kind, h]

        def weighted(h, p):
            return jnp.dot(v_ref[h, :, pl.ds(t0, win)], p.astype(BF16), preferred_element_type=F32)

        return logits, weighted

    def store(j, pvs):
        outs = [pv[:HEAD_DIM] / pv[HEAD_DIM:HEAD_DIM + 1] for pv in pvs]
        o_ref[j * qn:(j + 1) * qn, :] = jnp.concatenate(outs, axis=0).T.astype(BF16)

    sums = []
    for j in range(per_step):
        logits, weighted = block(j)
        sts = [logits(h) for h in range(nh)]
        pvs = [weighted(h, jnp.exp(st)) for h, st in enumerate(sts)]
        store(j, pvs)
        sums += [pv[HEAD_DIM:HEAD_DIM + 1] for pv in pvs]

    sums_ok = jnp.min(functools.reduce(jnp.minimum, sums)) >= NA_MIN_SUM

    @pl.when(jnp.logical_not(sums_ok))
    def _():
        for j in range(per_step):
            logits, weighted = block(j)
            safe = []
            for h in range(nh):
                st = logits(h)
                safe.append(weighted(h, jnp.exp(st - jnp.max(st, axis=0, keepdims=True))))
            store(j, safe)


def _na(qb_t, kb, vb_t, bias_tab):
    b, nh, _, s = qb_t.shape
    qn = NA_BLOCK_ROWS * GRID_W
    nblk = s // qn
    assert nblk >= 3
    per_step = NA_STEP_BLOCKS if nblk % NA_STEP_BLOCKS == 0 else 1
    qn *= per_step
    return pl.pallas_call(
        functools.partial(_na_kernel, nblk=nblk),
        grid=(b, nblk // per_step),
        in_specs=[
            pl.BlockSpec((None, nh, HEAD_DIM, qn), lambda bi, rb: (bi, 0, 0, rb)),
            _resident((None, nh // 2, s, 2 * HEAD_DIM), lambda bi, rb: (bi, 0, 0, 0)),
            _resident((None, nh, NA_VROWS, s), lambda bi, rb: (bi, 0, 0, 0)),
            _resident(bias_tab.shape, lambda bi, rb: (0, 0, 0, 0)),
        ],
        out_specs=pl.BlockSpec((None, qn, nh * HEAD_DIM), lambda bi, rb: (bi, rb, 0)),
        out_shape=jax.ShapeDtypeStruct((b, s, nh * HEAD_DIM), BF16),
        compiler_params=_params("parallel", "arbitrary"),
        name="nbr_attn",
    )(qb_t, kb, vb_t, bias_tab)


def _tile(s, want):
    return want if s % want == 0 else s


def kernel(x, c, ada_w, ada_b, norm_g, ff_w13, ff_w2, w_in, w_out, a_qk_norm, a_lambda, a_subln,
           b_qk_norm, b_rpb, pool_w, pool_b, pool_scale):
    b, s, d = x.shape
    depth = ada_w.shape[0]
    tm = _tile(s, TOKEN_TILE)
    ffn_tm = _tile(s, FFN_TOKEN_TILE)
    mod = _adaln(c, ada_w, ada_b)

    pos = jnp.arange(s, dtype=F32)
    inv = ROPE_THETA ** (-jnp.arange(0, HEAD_DIM, 2, dtype=F32) / HEAD_DIM)
    ang = inv[:, None] * pos[None, :]
    cos_t, sin_t = jnp.cos(ang), jnp.sin(ang)

    first = (ff_w13[0, :1].astype(BF16), ff_w2[0, :1].astype(BF16))
    rest = None

    def ffn_weights(layer, half):
        idx = 2 * layer + half
        return (*first, 0) if idx == 0 else (*rest, idx - 1)

    for layer in range(depth):
        m = mod[layer]
        x = _ffn(x, m, norm_g[layer, 0], *ffn_weights(layer, 0), ffn_tm)
        if layer % 2 == 0:
            e = layer // 2
            lambda_init = 0.8 - 0.6 * math.exp(-0.3 * layer)
            gains = jnp.stack([a_qk_norm[e, 0], a_qk_norm[e, 1], b_qk_norm[e, 0], b_qk_norm[e, 1]])
            gains = jnp.broadcast_to(gains[:, :, None], (4, HEAD_DIM, tm))
            qa_t, ka, va_t, qb_t, kb, vb_t = _proj(x, m, norm_g[layer, 1], w_in[e].T.astype(BF16),
                                                   gains, cos_t, sin_t, tm)
            mix_a, w13_rest, w2_rest = _dattn(qa_t, ka, va_t, a_lambda[e], a_qk_norm[e], a_subln[e], lambda_init,
                                              _tile(s, DA_QBLK), ff_w13, ff_w2)
            if rest is None:
                rest = (w13_rest, w2_rest)
            mix_b = _na(qb_t, kb, vb_t, _na_bias_table(b_rpb[e], b_qk_norm[e]))
            x = _attn_mix_ffn(x, mix_a, mix_b, w_out[e].astype(BF16), m, norm_g[layer, 2],
                              *ffn_weights(layer, 1), ffn_tm)
        else:
            o = layer // 2
            x = _pool_mix_ffn(x, norm_g[layer, 1], pool_w[o].astype(BF16), pool_b[o], pool_scale[o],
                              m, norm_g[layer, 2], *ffn_weights(layer, 1), ffn_tm)
    return x
```

```python
import functools
import math

import numpy as np
import jax
import jax.numpy as jnp
from jax import lax
from jax.experimental import pallas as pl
from jax.experimental.pallas import tpu as pltpu

F32 = jnp.float32
BF16 = jnp.bfloat16

D_MODEL = 1024
HEAD_DIM = 64
A_HEADS = 4
A_VDIM = 2 * HEAD_DIM
B_HEADS = 8
D_FF = 2816
GRID_W = 64
NA_ROWS = 8
NA_COLS = 16
POOL_WINDOWS = (2, 4, 8, 16)
POOL_GROUP_DIM = D_MODEL // len(POOL_WINDOWS)
ROPE_THETA = 10000.0
EPS = 1e-6
N_MOD = 9
NEG_INF = -1e30
QK_SCALE = HEAD_DIM ** -0.5

V7X_VMEM_LIMIT_BYTES = 56 * 1024 * 1024
BF16_SUBLANES = 16
ADALN_K_TILE = 256
TOKEN_TILE = 1024
PROJ_TOKEN_SUB = 256
FFN_TOKEN_TILE = 1024
FFN_ROW_BLOCK = 256
FF_CHUNK = 256
DA_QBLK = 1024
DA_QSUB = 512
DA_KBLK = 4096
DA_BOUND_MARGIN = 1.02
DA_MIN_SUM = 1e-18
DA_SAFE_KBLK = 256
DA_VROWS = A_VDIM + BF16_SUBLANES
NA_BLOCK_ROWS = 4
NA_WIN_ROWS = 3 * NA_BLOCK_ROWS
NA_STEP_BLOCKS = 4
NA_VROWS = HEAD_DIM + BF16_SUBLANES
NA_DR_PAD = 16
NA_MIN_SUM = 1e-18
POOL_HALO = 8


def _params(*sem):
    return pltpu.CompilerParams(dimension_semantics=sem, vmem_limit_bytes=V7X_VMEM_LIMIT_BYTES)


def _resident(shape, index_map):
    return pl.BlockSpec(shape, index_map, pipeline_mode=pl.Buffered(1))


def _norm_mod(x, g, shift, scale):
    ms = jnp.mean(x * x, axis=-1, keepdims=True)
    y = (x * lax.rsqrt(ms + EPS)) * g
    return y * (1.0 + scale) + shift


def _adaln_kernel(c_ref, w_ref, b_ref, o_ref):
    tk = w_ref.shape[0]
    k = pl.program_id(1)
    c = c_ref[:, pl.ds(pl.multiple_of(k * tk, tk), tk)]
    cond = c / (1.0 + jnp.exp(-c))

    @pl.when(k == 0)
    def _():
        o_ref[...] = jnp.broadcast_to(b_ref[...], o_ref.shape)

    o_ref[...] += jnp.dot(cond.astype(BF16), w_ref[...].astype(BF16), preferred_element_type=F32)


def _adaln(c, ada_w, ada_b):
    depth, d, n = ada_w.shape
    b = c.shape[0]
    bp = -(-b // 8) * 8
    cp = jnp.pad(c, ((0, bp - b), (0, 0)))
    tk = ADALN_K_TILE
    out = pl.pallas_call(
        _adaln_kernel,
        grid=(depth, d // tk),
        in_specs=[
            pl.BlockSpec((bp, d), lambda l, k: (0, 0)),
            pl.BlockSpec((None, tk, n), lambda l, k: (l, k, 0)),
            pl.BlockSpec((None, 1, n), lambda l, k: (l, 0, 0)),
        ],
        out_specs=pl.BlockSpec((None, bp, n), lambda l, k: (l, 0, 0)),
        out_shape=jax.ShapeDtypeStruct((depth, bp, n), F32),
        compiler_params=_params("arbitrary", "arbitrary"),
        name="adaln",
    )(cp, ada_w, ada_b.reshape(depth, 1, n))
    return out[:, :b].reshape(depth, b, N_MOD, d)


def _swiglu_tile(prep, x_rows, tm, mod_ref, g_ref, w13_ref, w2_ref, o_ref, act_ref, h_buf, row0):
    rows = FFN_ROW_BLOCK if tm % FFN_ROW_BLOCK == 0 else tm
    n_blocks = tm // rows
    n_chunks = D_FF // FF_CHUNK
    gate = 0.5 * mod_ref[row0 + 2:row0 + 3, :]

    def norm_step(k, r0):
        def run():
            h_buf[k % 2] = _norm_mod(x_rows(k, r0, rows), g_ref[...], mod_ref[row0:row0 + 1, :],
                                     mod_ref[row0 + 1:row0 + 2, :]).astype(BF16)
        return run

    steps = [list(prep(k, k * rows, rows)) + [norm_step(k, k * rows)] for k in range(n_blocks)]
    for step in steps[0]:
        step()
    for k in range(n_blocks):
        r0 = k * rows
        side = steps[k + 1] if k + 1 < n_blocks else []
        after = {((j + 1) * n_chunks) // (len(side) + 1) - 1: step for j, step in enumerate(side)}
        assert len(after) == len(side)
        h = h_buf[k % 2]
        for ci in range(n_chunks):
            lo = ci * FF_CHUNK
            a = jnp.dot(h, w13_ref[:, lo:lo + FF_CHUNK], preferred_element_type=F32)
            b = jnp.dot(h, w13_ref[:, D_FF + lo:D_FF + lo + FF_CHUNK], preferred_element_type=F32)
            act_ref[r0:r0 + rows, lo:lo + FF_CHUNK] = ((a / (1.0 + jnp.exp(-a))) * b).astype(BF16)
            if ci in after:
                after[ci]()
        y = jnp.dot(act_ref[r0:r0 + rows, :], w2_ref[...], preferred_element_type=F32)
        o_ref[r0:r0 + rows, :] = x_rows(k, r0, rows) + gate * y


def _ffn_kernel(x_ref, mod_ref, g_ref, w13_ref, w2_ref, o_ref, act_ref, h_buf):
    _swiglu_tile(lambda k, r0, rows: (), lambda k, r0, rows: x_ref[r0:r0 + rows, :], x_ref.shape[0],
                 mod_ref, g_ref, w13_ref, w2_ref, o_ref, act_ref, h_buf, 0)


def _attn_mix_ffn_kernel(x_ref, ma_ref, mb_ref, wo_ref, mod_ref, g_ref, w13_ref, w2_ref, o_ref,
                         act_ref, h_buf, x1_buf):
    half = ma_ref.shape[1]

    def prep(k, r0, rows):
        def mix():
            y = (jnp.dot(ma_ref[r0:r0 + rows, :], wo_ref[:half, :], preferred_element_type=F32)
                 + jnp.dot(mb_ref[r0:r0 + rows, :], wo_ref[half:, :], preferred_element_type=F32))
            x1_buf[k % 2] = x_ref[r0:r0 + rows, :] + mod_ref[5:6, :] * y
        return [mix]

    _swiglu_tile(prep, lambda k, r0, rows: x1_buf[k % 2], x_ref.shape[0],
                 mod_ref, g_ref, w13_ref, w2_ref, o_ref, act_ref, h_buf, 6)


def _pool_mix_ffn_kernel(x_ref, xp_ref, xn_ref, gm_ref, pw_ref, pb_ref, ps_ref, mod_ref, g_ref, w13_ref, w2_ref,
                         o_ref, act_ref, h_buf, x1_buf, *, s_len):
    tm = x_ref.shape[0]
    i = pl.program_id(1)

    def count(t, w):
        return (jnp.minimum(t + w // 2, s_len) - jnp.maximum(t - w // 2, 0)).astype(F32)

    def prep(k, r0, rows):
        if r0 == 0:
            head, head_on = xp_ref, (i > 0).astype(F32)
        else:
            head, head_on = x_ref.at[r0 - POOL_HALO:r0, :], 1.0
        if r0 + rows == tm:
            tail, tail_on = xn_ref, (i < pl.num_programs(1) - 1).astype(F32)
        else:
            tail, tail_on = x_ref.at[r0 + rows:r0 + rows + POOL_HALO, :], 1.0
        body = x_ref.at[r0:r0 + rows, :]
        inv_rms = {}

        def stats():
            for name, ref in (("head", head), ("body", body), ("tail", tail)):
                xs = ref[...]
                inv_rms[name] = lax.rsqrt(jnp.mean(xs * xs, axis=-1, keepdims=True) + EPS)

        def group_step(gi, w):
            def run():
                cols = slice(gi * POOL_GROUP_DIM, (gi + 1) * POOL_GROUP_DIM)
                gain, shift, scale = gm_ref[:, cols], mod_ref[3:4, cols], mod_ref[4:5, cols]

                def part(name, ref):
                    return ((ref[:, cols] * inv_rms[name]) * gain) * (1.0 + scale) + shift

                h = part("body", body)
                run_sum = jnp.concatenate([part("head", head) * head_on, h, part("tail", tail) * tail_on], axis=0)
                span = 1
                while span < w:
                    n = run_sum.shape[0]
                    run_sum = run_sum[:n - span] + run_sum[span:]
                    span *= 2
                start = POOL_HALO - w // 2
                seg = run_sum[start:start + rows]
                t_head = i * tm + r0 + lax.broadcasted_iota(jnp.int32, (POOL_HALO, 1), 0)
                t_tail = t_head + (rows - POOL_HALO)
                pooled = jnp.concatenate([seg[:POOL_HALO] / count(t_head, w),
                                          seg[POOL_HALO:rows - POOL_HALO] * (1.0 / w),
                                          seg[rows - POOL_HALO:] / count(t_tail, w)], axis=0)
                y = jnp.dot((pooled - h).astype(BF16), pw_ref[gi], preferred_element_type=F32)
                y = (y + pb_ref[:, cols]) * ps_ref[:, cols]
                x1_buf[k % 2, :, cols] = body[:, cols] + mod_ref[5:6, cols] * y
            return run

        return [stats] + [group_step(gi, w) for gi, w in enumerate(POOL_WINDOWS)]

    _swiglu_tile(prep, lambda k, r0, rows: x1_buf[k % 2], tm,
                 mod_ref, g_ref, w13_ref, w2_ref, o_ref, act_ref, h_buf, 6)


def _ffn_call(body, name, x, mixer_inputs, mixer_specs, mod, g, w13, w2, widx, tm):
    b, s, d = x.shape
    rows = FFN_ROW_BLOCK if tm % FFN_ROW_BLOCK == 0 else tm
    row = pl.BlockSpec((1, d), lambda bi, i: (0, 0))
    scratch = [pltpu.VMEM((tm, D_FF), BF16), pltpu.VMEM((2, rows, d), BF16)]
    if mixer_inputs:
        scratch.append(pltpu.VMEM((2, rows, d), F32))
    return pl.pallas_call(
        body,
        grid=(b, s // tm),
        in_specs=[pl.BlockSpec((None, tm, d), lambda bi, i: (bi, i, 0))] + list(mixer_specs) + [
            pl.BlockSpec((None, N_MOD, d), lambda bi, i: (bi, 0, 0)),
            row,
            _resident((None, d, 2 * D_FF), lambda bi, i: (widx, 0, 0)),
            _resident((None, D_FF, d), lambda bi, i: (widx, 0, 0)),
        ],
        out_specs=pl.BlockSpec((None, tm, d), lambda bi, i: (bi, i, 0)),
        out_shape=jax.ShapeDtypeStruct(x.shape, F32),
        scratch_shapes=scratch,
        compiler_params=_params("parallel", "parallel"),
        name=name,
    )(x, *mixer_inputs, mod, g.reshape(1, d), w13, w2)


def _ffn(x, mod, g, w13, w2, widx, tm):
    return _ffn_call(_ffn_kernel, "ffn", x, (), (), mod, g, w13, w2, widx, tm)


def _attn_mix_ffn(x, mix_a, mix_b, w_out, mod, g, w13, w2, widx, tm):
    half = mix_a.shape[-1]
    mix = pl.BlockSpec((None, tm, half), lambda bi, i: (bi, i, 0))
    specs = [mix, mix, _resident(w_out.shape, lambda bi, i: (0, 0))]
    return _ffn_call(_attn_mix_ffn_kernel, "attn_mix_ffn", x, (mix_a, mix_b, w_out), specs, mod, g, w13, w2, widx, tm)


def _pool_mix_ffn(x, g_mix, pool_w, pool_b, pool_scale, mod, g, w13, w2, widx, tm):
    b, s, d = x.shape
    per = tm // POOL_HALO
    last = s // POOL_HALO - 1
    row = pl.BlockSpec((1, d), lambda bi, i: (0, 0))
    specs = [
        pl.BlockSpec((None, POOL_HALO, d), lambda bi, i: (bi, jnp.maximum(i * per - 1, 0), 0)),
        pl.BlockSpec((None, POOL_HALO, d), lambda bi, i: (bi, jnp.minimum((i + 1) * per, last), 0)),
        row,
        pl.BlockSpec(pool_w.shape, lambda bi, i: (0, 0, 0)),
        row,
        row,
    ]
    inputs = (x, x, g_mix.reshape(1, d), pool_w, pool_b.reshape(1, d), pool_scale.reshape(1, d))
    return _ffn_call(functools.partial(_pool_mix_ffn_kernel, s_len=s), "pool_mix_ffn", x, inputs, specs,
                     mod, g, w13, w2, widx, tm)


def _proj_kernel(x_ref, mod_ref, g_ref, wt_ref, gain_ref, cos_ref, sin_ref,
                 qa_ref, ka_ref, va_ref, qb_ref, kb_ref, vb_ref, h_buf):
    tm = x_ref.shape[0]
    ts = PROJ_TOKEN_SUB if tm % PROJ_TOKEN_SUB == 0 else tm
    width = B_HEADS * HEAD_DIM

    def norm_step(j):
        h_buf[j % 2] = _norm_mod(x_ref[j * ts:(j + 1) * ts, :], g_ref[...], mod_ref[3:4, :],
                                 mod_ref[4:5, :]).astype(BF16)

    norm_step(0)
    for j in range(tm // ts):
        tok = slice(j * ts, (j + 1) * ts)
        h = h_buf[j % 2]
        cos = cos_ref[:, tok]
        sin = sin_ref[:, tok]

        def group_t(gi):
            return lax.dot_general(wt_ref[gi * width:(gi + 1) * width, :], h,
                                   (((1,), (1,)), ((), ())), preferred_element_type=F32)

        def head_norm(ut, gain):
            x3 = ut.reshape(B_HEADS, HEAD_DIM, ts)
            ms = jnp.mean(x3 * x3, axis=1, keepdims=True)
            return (x3 * lax.rsqrt(ms + EPS)) * gain[None]

        def rope(x3):
            x1 = x3[:, :HEAD_DIM // 2, :]
            x2 = x3[:, HEAD_DIM // 2:, :]
            return jnp.concatenate([x1 * cos - x2 * sin, x1 * sin + x2 * cos], axis=1)

        qa = rope(head_norm(group_t(0), gain_ref[0, :, tok])) * QK_SCALE
        qa_ref[:, :, tok] = qa.reshape(A_HEADS, A_VDIM, ts).astype(BF16)
        ka = rope(head_norm(group_t(1), gain_ref[1, :, tok])).reshape(A_HEADS, A_VDIM, ts)
        for hh in range(A_HEADS):
            ka_ref[hh, tok, :] = ka[hh].T.astype(BF16)
        if (j + 1) * ts < tm:
            norm_step(j + 1)
        qb = head_norm(group_t(3), gain_ref[2, :, tok]) * QK_SCALE
        qb_ref[:, :, tok] = qb.astype(BF16)
        kb = head_norm(group_t(4), gain_ref[3, :, tok]).reshape(B_HEADS // 2, 2 * HEAD_DIM, ts)
        for hp in range(B_HEADS // 2):
            kb_ref[hp, tok, :] = kb[hp].T.astype(BF16)
        va_ref[:, :A_VDIM, tok] = group_t(2).reshape(A_HEADS, A_VDIM, ts).astype(BF16)
        va_ref[:, A_VDIM:, tok] = jnp.ones((A_HEADS, DA_VROWS - A_VDIM, ts), BF16)
        vb_ref[:, :HEAD_DIM, tok] = group_t(5).reshape(B_HEADS, HEAD_DIM, ts).astype(BF16)
        vb_ref[:, HEAD_DIM:, tok] = jnp.ones((B_HEADS, NA_VROWS - HEAD_DIM, ts), BF16)


def _proj(x, mod, g, w_in_t, gains, cos_t, sin_t, tm):
    b, s, d = x.shape
    tok = lambda bi, i: (bi, 0, 0, i)
    seq = lambda bi, i: (bi, 0, i, 0)
    return pl.pallas_call(
        _proj_kernel,
        grid=(b, s // tm),
        in_specs=[
            pl.BlockSpec((None, tm, d), lambda bi, i: (bi, i, 0)),
            pl.BlockSpec((None, N_MOD, d), lambda bi, i: (bi, 0, 0)),
            pl.BlockSpec((1, d), lambda bi, i: (0, 0)),
            _resident(w_in_t.shape, lambda bi, i: (0, 0)),
            pl.BlockSpec((4, HEAD_DIM, tm), lambda bi, i: (0, 0, 0)),
            pl.BlockSpec((HEAD_DIM // 2, tm), lambda bi, i: (0, i)),
            pl.BlockSpec((HEAD_DIM // 2, tm), lambda bi, i: (0, i)),
        ],
        out_specs=[
            pl.BlockSpec((None, A_HEADS, A_VDIM, tm), tok),
            pl.BlockSpec((None, A_HEADS, tm, A_VDIM), seq),
            pl.BlockSpec((None, A_HEADS, DA_VROWS, tm), tok),
            pl.BlockSpec((None, B_HEADS, HEAD_DIM, tm), tok),
            pl.BlockSpec((None, B_HEADS // 2, tm, 2 * HEAD_DIM), seq),
            pl.BlockSpec((None, B_HEADS, NA_VROWS, tm), tok),
        ],
        out_shape=[
            jax.ShapeDtypeStruct((b, A_HEADS, A_VDIM, s), BF16),
            jax.ShapeDtypeStruct((b, A_HEADS, s, A_VDIM), BF16),
            jax.ShapeDtypeStruct((b, A_HEADS, DA_VROWS, s), BF16),
            jax.ShapeDtypeStruct((b, B_HEADS, HEAD_DIM, s), BF16),
            jax.ShapeDtypeStruct((b, B_HEADS // 2, s, 2 * HEAD_DIM), BF16),
            jax.ShapeDtypeStruct((b, B_HEADS, NA_VROWS, s), BF16),
        ],
        scratch_shapes=[pltpu.VMEM((2, PROJ_TOKEN_SUB if tm % PROJ_TOKEN_SUB == 0 else tm, d), BF16)],
        compiler_params=_params("parallel", "parallel"),
        name="qkv_proj",
    )(x, mod, g.reshape(1, d), w_in_t, gains, cos_t, sin_t)


def _dattn_kernel(q_ref, k_ref, v_ref, lam_ref, gn_ref, sg_ref, w13f_ref, w2f_ref, o_ref, w13b_ref, w2b_ref,
                  acc_ref, *, kblk, lambda_init):
    s_len = k_ref.shape[0]
    n_strips = q_ref.shape[1] // DA_QSUB
    lp = lam_ref[...]
    lam = (jnp.exp(jnp.sum(lp[0:1] * lp[1:2], axis=-1, keepdims=True))
           - jnp.exp(jnp.sum(lp[2:3] * lp[3:4], axis=-1, keepdims=True)) + lambda_init)
    gmax = jnp.max(jnp.abs(gn_ref[...]), axis=-1, keepdims=True)
    shift = (DA_BOUND_MARGIN * HEAD_DIM * QK_SCALE) * gmax[0:1] * gmax[1:2]
    z = jnp.zeros((HEAD_DIM, DA_QSUB), BF16)

    def strip_rhs(sb):
        qt = q_ref[:, sb * DA_QSUB:(sb + 1) * DA_QSUB]
        return jnp.concatenate([jnp.concatenate([qt[:HEAD_DIM], z], axis=0),
                                jnp.concatenate([z, qt[HEAD_DIM:]], axis=0)], axis=1)

    def finish(sb, num, den):
        o = num / den
        ot = o[:, :DA_QSUB] - lam * o[:, DA_QSUB:]
        ms = jnp.mean(ot * ot, axis=0, keepdims=True)
        y = ((ot * lax.rsqrt(ms + EPS)) * sg_ref[...]) * (1.0 - lambda_init)
        o_ref[sb * DA_QSUB:(sb + 1) * DA_QSUB, :] = y.T.astype(BF16)

    min_sum = None
    for sb in range(n_strips):
        if sb == n_strips // 2:
            w13b_ref[...] = w13f_ref[...].astype(BF16)
            w2b_ref[...] = w2f_ref[...].astype(BF16)
        rhs = strip_rhs(sb)
        num = None
        den8 = None
        for k0 in range(0, s_len, kblk):
            st = jnp.dot(k_ref[k0:k0 + kblk, :], rhs, preferred_element_type=F32)
            p = jnp.exp(st - shift)
            part = jnp.sum(p.reshape(kblk // 8, 8, 2 * DA_QSUB), axis=0)
            pv = jnp.dot(v_ref[:A_VDIM, k0:k0 + kblk], p.astype(BF16), preferred_element_type=F32)
            num = pv if num is None else num + pv
            den8 = part if den8 is None else den8 + part
        den = jnp.sum(den8, axis=0, keepdims=True)
        finish(sb, num, den)
        min_sum = den if min_sum is None else jnp.minimum(min_sum, den)

    sums_ok = jnp.min(min_sum) >= DA_MIN_SUM

    @pl.when(jnp.logical_not(sums_ok))
    def _():
        for sb in range(n_strips):
            rhs = strip_rhs(sb)
            acc_ref[...] = jnp.zeros_like(acc_ref)

            def safe_body(t, m):
                k0 = pl.multiple_of(t * DA_SAFE_KBLK, DA_SAFE_KBLK)
                st = jnp.dot(k_ref[pl.ds(k0, DA_SAFE_KBLK), :], rhs, preferred_element_type=F32)
                m_new = jnp.maximum(m, jnp.max(st, axis=0, keepdims=True))
                pv = jnp.dot(v_ref[:, pl.ds(k0, DA_SAFE_KBLK)], jnp.exp(st - m_new).astype(BF16),
                             preferred_element_type=F32)
                acc_ref[...] = acc_ref[...] * jnp.exp(m - m_new) + pv
                return m_new

            lax.fori_loop(0, s_len // DA_SAFE_KBLK, safe_body, jnp.full((1, 2 * DA_QSUB), NEG_INF, F32))
            finish(sb, acc_ref[:A_VDIM, :], acc_ref[A_VDIM:A_VDIM + 1, :])


def _cast_rows(total, steps):
    return min(r for r in range(BF16_SUBLANES, total + 1, BF16_SUBLANES) if total % r == 0 and total // r <= steps)


def _dattn(qa_t, ka, va_t, a_lambda, a_qk_gain, a_subln, lambda_init, qblk, ff_w13, ff_w2):
    b, nh, _, s = qa_t.shape
    assert qblk % DA_QSUB == 0
    nq = s // qblk
    depth, halves, d, f2 = ff_w13.shape
    n_sets = depth * halves - 1
    per_set = (b * nh * nq) // n_sets
    assert per_set >= 1
    rows13, rows2 = _cast_rows(d, per_set), _cast_rows(f2 // 2, per_set)

    def cast_block(rows_total, rows):
        nblk = rows_total // rows

        def locate(bi, h, i):
            blk = jnp.minimum((bi * nh + h) * nq + i, n_sets * nblk - 1)
            return blk // nblk, blk % nblk

        def src(bi, h, i):
            st, rb = locate(bi, h, i)
            return ((st + 1) // halves, (st + 1) % halves, rb, 0)

        def dst(bi, h, i):
            st, rb = locate(bi, h, i)
            return (st, rb, 0)

        return src, dst

    src13, dst13 = cast_block(d, rows13)
    src2, dst2 = cast_block(f2 // 2, rows2)
    return pl.pallas_call(
        functools.partial(_dattn_kernel, kblk=_tile(s, DA_KBLK), lambda_init=lambda_init),
        grid=(b, nh, nq),
        in_specs=[
            pl.BlockSpec((None, None, A_VDIM, qblk), lambda bi, h, i: (bi, h, 0, i)),
            pl.BlockSpec((None, None, s, A_VDIM), lambda bi, h, i: (bi, h, 0, 0)),
            pl.BlockSpec((None, None, DA_VROWS, s), lambda bi, h, i: (bi, h, 0, 0)),
            pl.BlockSpec((4, HEAD_DIM), lambda bi, h, i: (0, 0)),
            pl.BlockSpec((2, HEAD_DIM), lambda bi, h, i: (0, 0)),
            pl.BlockSpec((A_VDIM, DA_QSUB), lambda bi, h, i: (0, 0)),
            pl.BlockSpec((None, None, rows13, f2), src13),
            pl.BlockSpec((None, None, rows2, d), src2),
        ],
        out_specs=[
            pl.BlockSpec((None, qblk, A_VDIM), lambda bi, h, i: (bi, i, h)),
            pl.BlockSpec((None, rows13, f2), dst13),
            pl.BlockSpec((None, rows2, d), dst2),
        ],
        out_shape=[
            jax.ShapeDtypeStruct((b, s, nh * A_VDIM), BF16),
            jax.ShapeDtypeStruct((n_sets, d, f2), BF16),
            jax.ShapeDtypeStruct((n_sets, f2 // 2, d), BF16),
        ],
        scratch_shapes=[pltpu.VMEM((DA_VROWS, 2 * DA_QSUB), F32)],
        compiler_params=_params("arbitrary", "arbitrary", "arbitrary"),
        name="diff_attn",
    )(qa_t, ka, va_t, a_lambda, a_qk_gain, jnp.broadcast_to(a_subln[:, None], (A_VDIM, DA_QSUB)), ff_w13, ff_w2)


def _toeplitz_selector():
    kc = np.arange(GRID_W)[:, None]
    qc = np.arange(GRID_W)[None, :]
    cs = np.clip(qc - NA_COLS // 2, 0, GRID_W - NA_COLS)
    col_ok = (kc >= cs) & (kc < cs + NA_COLS)
    dc = np.clip(kc - qc + NA_COLS - 1, 0, 2 * NA_COLS - 2)
    sel = (np.arange(2 * NA_COLS)[:, None, None] == dc[None]) & col_ok[None]
    mask = np.where(col_ok, 0.0, NEG_INF)
    return (sel.reshape(2 * NA_COLS, GRID_W * GRID_W).astype(np.float32),
            mask.reshape(1, GRID_W * GRID_W).astype(np.float32))


def _rpb_expand_kernel(r_ref, gn_ref, sel_ref, mask_ref, o_ref, *, ndr, ndc):
    r = r_ref[...]
    nh = r.shape[0] // NA_DR_PAD
    rows = lax.broadcasted_iota(jnp.int32, r.shape, 0)
    cols = lax.broadcasted_iota(jnp.int32, r.shape, 1)
    real = ((rows & (NA_DR_PAD - 1)) < ndr) & (cols < ndc)
    row_max = jnp.max(jnp.where(real, r, NEG_INF), axis=1, keepdims=True)
    head_max = jnp.max(row_max.reshape(nh, NA_DR_PAD, 1), axis=1, keepdims=True)
    gmax = jnp.max(jnp.abs(gn_ref[...]), axis=-1, keepdims=True)
    bound = (DA_BOUND_MARGIN * HEAD_DIM * QK_SCALE) * gmax[0:1] * gmax[1:2]
    shift = jnp.broadcast_to(head_max + bound[None], (nh, NA_DR_PAD, 1)).reshape(r.shape[0], 1)
    o_ref[...] = (jnp.dot(r, sel_ref[...], preferred_element_type=F32, precision=lax.Precision.HIGHEST)
                  + mask_ref[...]) - shift


def _na_bias_table(rpb, qk_gain):
    nh, ndr, ndc = rpb.shape
    sel, mask = _toeplitz_selector()
    r = jnp.pad(rpb, ((0, 0), (0, NA_DR_PAD - ndr), (0, sel.shape[0] - ndc))).reshape(nh * NA_DR_PAD, sel.shape[0])
    flat = pl.pallas_call(
        functools.partial(_rpb_expand_kernel, ndr=ndr, ndc=ndc),
        out_shape=jax.ShapeDtypeStruct((nh * NA_DR_PAD, GRID_W * GRID_W), F32),
        name="rpb_expand",
    )(r, qk_gain, sel, mask)
    tile = flat.reshape(nh, NA_DR_PAD, GRID_W, GRID_W)
    neg = jnp.full((nh, GRID_W, GRID_W), NEG_INF, F32)
    half = NA_ROWS // 2
    kinds = []
    for kind in range(3):
        win_rows = []
        for i in range(NA_WIN_ROWS):
            blocks = []
            for j in range(NA_BLOCK_ROWS):
                if kind == 0:
                    dr, ok = i - j + NA_ROWS - 1, i < NA_ROWS
                elif kind == 1:
                    dr, ok = i - j + half - 1, j <= i < j + NA_ROWS
                else:
                    dr, ok = i - j - 1, i >= NA_WIN_ROWS - NA_ROWS
                blocks.append(tile[:, dr] if ok else neg)
            win_rows.append(jnp.concatenate(blocks, axis=-1))
        kinds.append(jnp.concatenate(win_rows, axis=1))
    return jnp.stack(kinds)


def _na_kernel(q_ref, k_ref, v_ref, bias_ref, o_ref, *, nblk):
    qn = NA_BLOCK_ROWS * GRID_W
    win = NA_WIN_ROWS * GRID_W
    nh = q_ref.shape[0]
    per_step = q_ref.shape[2] // qn
    z = jnp.zeros((HEAD_DIM, qn), BF16)

    def block(j):
        rb = pl.program_id(1) * per_step + j
        t0 = pl.multiple_of(jnp.clip(rb - 1, 0, nblk - 3) * qn, qn)
        kind = jnp.where(rb == 0, 0, jnp.where(rb == nblk - 1, 2, 1))

        def logits(h):
            qt = q_ref[h, :, j * qn:(j + 1) * qn]
            rhs = jnp.concatenate([qt, z] if h % 2 == 0 else [z, qt], axis=0)
            kwin = k_ref[h // 2, pl.ds(t0, win), :]
            return jnp.dot(kwin, rhs, preferred_element_type=F32) + bias_ref[kind, h]

        def weighted(h, p):
            return jnp.dot(v_ref[h, :, pl.ds(t0, win)], p.astype(BF16), preferred_element_type=F32)

        return logits, weighted

    def store(j, pvs):
        outs = [pv[:HEAD_DIM] / pv[HEAD_DIM:HEAD_DIM + 1] for pv in pvs]
        o_ref[j * qn:(j + 1) * qn, :] = jnp.concatenate(outs, axis=0).T.astype(BF16)

    sums = []
    for j in range(per_step):
        logits, weighted = block(j)
        sts = [logits(h) for h in range(nh)]
        pvs = [weighted(h, jnp.exp(st)) for h, st in enumerate(sts)]
        store(j, pvs)
        sums += [pv[HEAD_DIM:HEAD_DIM + 1] for pv in pvs]

    sums_ok = jnp.min(functools.reduce(jnp.minimum, sums)) >= NA_MIN_SUM

    @pl.when(jnp.logical_not(sums_ok))
    def _():
        for j in range(per_step):
            logits, weighted = block(j)
            safe = []
            for h in range(nh):
                st = logits(h)
                safe.append(weighted(h, jnp.exp(st - jnp.max(st, axis=0, keepdims=True))))
            store(j, safe)


def _na(qb_t, kb, vb_t, bias_tab):
    b, nh, _, s = qb_t.shape
    qn = NA_BLOCK_ROWS * GRID_W
    nblk = s // qn
    assert nblk >= 3
    per_step = NA_STEP_BLOCKS if nblk % NA_STEP_BLOCKS == 0 else 1
    qn *= per_step
    return pl.pallas_call(
        functools.partial(_na_kernel, nblk=nblk),
        grid=(b, nblk // per_step),
        in_specs=[
            pl.BlockSpec((None, nh, HEAD_DIM, qn), lambda bi, rb: (bi, 0, 0, rb)),
            _resident((None, nh // 2, s, 2 * HEAD_DIM), lambda bi, rb: (bi, 0, 0, 0)),
            _resident((None, nh, NA_VROWS, s), lambda bi, rb: (bi, 0, 0, 0)),
            _resident(bias_tab.shape, lambda bi, rb: (0, 0, 0, 0)),
        ],
        out_specs=pl.BlockSpec((None, qn, nh * HEAD_DIM), lambda bi, rb: (bi, rb, 0)),
        out_shape=jax.ShapeDtypeStruct((b, s, nh * HEAD_DIM), BF16),
        compiler_params=_params("parallel", "arbitrary"),
        name="nbr_attn",
    )(qb_t, kb, vb_t, bias_tab)


def _tile(s, want):
    return want if s % want == 0 else s


def kernel(x, c, ada_w, ada_b, norm_g, ff_w13, ff_w2, w_in, w_out, a_qk_norm, a_lambda, a_subln,
           b_qk_norm, b_rpb, pool_w, pool_b, pool_scale):
    b, s, d = x.shape
    depth = ada_w.shape[0]
    tm = _tile(s, TOKEN_TILE)
    ffn_tm = _tile(s, FFN_TOKEN_TILE)
    mod = _adaln(c, ada_w, ada_b)

    pos = jnp.arange(s, dtype=F32)
    inv = ROPE_THETA ** (-jnp.arange(0, HEAD_DIM, 2, dtype=F32) / HEAD_DIM)
    ang = inv[:, None] * pos[None, :]
    cos_t, sin_t = jnp.cos(ang), jnp.sin(ang)

    first = (ff_w13[0, :1].astype(BF16), ff_w2[0, :1].astype(BF16))
    rest = None

    def ffn_weights(layer, half):
        idx = 2 * layer + half
        return (*first, 0) if idx == 0 else (*rest, idx - 1)

    for layer in range(depth):
        m = mod[layer]
        x = _ffn(x, m, norm_g[layer, 0], *ffn_weights(layer, 0), ffn_tm)
        if layer % 2 == 0:
            e = layer // 2
            lambda_init = 0.8 - 0.6 * math.exp(-0.3 * layer)
            gains = jnp.stack([a_qk_norm[e, 0], a_qk_norm[e, 1], b_qk_norm[e, 0], b_qk_norm[e, 1]])
            gains = jnp.broadcast_to(gains[:, :, None], (4, HEAD_DIM, tm))
            qa_t, ka, va_t, qb_t, kb, vb_t = _proj(x, m, norm_g[layer, 1], w_in[e].T.astype(BF16),
                                                   gains, cos_t, sin_t, tm)
            mix_a, w13_rest, w2_rest = _dattn(qa_t, ka, va_t, a_lambda[e], a_qk_norm[e], a_subln[e], lambda_init,
                                              _tile(s, DA_QBLK), ff_w13, ff_w2)
            if rest is None:
                rest = (w13_rest, w2_rest)
            mix_b = _na(qb_t, kb, vb_t, _na_bias_table(b_rpb[e], b_qk_norm[e]))
            x = _attn_mix_ffn(x, mix_a, mix_b, w_out[e].astype(BF16), m, norm_g[layer, 2],
                              *ffn_weights(layer, 1), ffn_tm)
        else:
            o = layer // 2
            x = _pool_mix_ffn(x, norm_g[layer, 1], pool_w[o].astype(BF16), pool_b[o], pool_scale[o],
                              m, norm_g[layer, 2], *ffn_weights(layer, 1), ffn_tm)
    return x
```

```python
import functools
import math

import numpy as np
import jax
import jax.numpy as jnp
from jax import lax
from jax.experimental import pallas as pl
from jax.experimental.pallas import tpu as pltpu

F32 = jnp.float32
BF16 = jnp.bfloat16

D_MODEL = 1024
HEAD_DIM = 64
A_HEADS = 4
A_VDIM = 2 * HEAD_DIM
B_HEADS = 8
D_FF = 2816
GRID_W = 64
NA_ROWS = 8
NA_COLS = 16
POOL_WINDOWS = (2, 4, 8, 16)
POOL_GROUP_DIM = D_MODEL // len(POOL_WINDOWS)
ROPE_THETA = 10000.0
EPS = 1e-6
N_MOD = 9
NEG_INF = -1e30
QK_SCALE = HEAD_DIM ** -0.5

V7X_VMEM_LIMIT_BYTES = 56 * 1024 * 1024
BF16_SUBLANES = 16
ADALN_K_TILE = 256
TOKEN_TILE = 1024
PROJ_TOKEN_SUB = 256
FFN_TOKEN_TILE = 1024
FFN_ROW_BLOCK = 256
FF_CHUNK = 256
DA_QBLK = 1024
DA_QSUB = 1024
DA_KBLK = 2048
DA_BOUND_MARGIN = 1.02
DA_MIN_SUM = 1e-18
DA_SAFE_KBLK = 256
DA_VROWS = A_VDIM + BF16_SUBLANES
NA_BLOCK_ROWS = 4
NA_WIN_ROWS = 3 * NA_BLOCK_ROWS
NA_STEP_BLOCKS = 4
NA_VROWS = HEAD_DIM + BF16_SUBLANES
NA_DR_PAD = 16
NA_MIN_SUM = 1e-18
POOL_HALO = 8


def _params(*sem):
    return pltpu.CompilerParams(dimension_semantics=sem, vmem_limit_bytes=V7X_VMEM_LIMIT_BYTES)


def _resident(shape, index_map):
    return pl.BlockSpec(shape, index_map, pipeline_mode=pl.Buffered(1))


def _norm_mod(x, g, shift, scale):
    ms = jnp.mean(x * x, axis=-1, keepdims=True)
    y = (x * lax.rsqrt(ms + EPS)) * g
    return y * (1.0 + scale) + shift


def _adaln_kernel(c_ref, w_ref, b_ref, o_ref):
    tk = w_ref.shape[0]
    k = pl.program_id(1)
    c = c_ref[:, pl.ds(pl.multiple_of(k * tk, tk), tk)]
    cond = c / (1.0 + jnp.exp(-c))

    @pl.when(k == 0)
    def _():
        o_ref[...] = jnp.broadcast_to(b_ref[...], o_ref.shape)

    o_ref[...] += jnp.dot(cond.astype(BF16), w_ref[...].astype(BF16), preferred_element_type=F32)


def _adaln(c, ada_w, ada_b):
    depth, d, n = ada_w.shape
    b = c.shape[0]
    bp = -(-b // 8) * 8
    cp = jnp.pad(c, ((0, bp - b), (0, 0)))
    tk = ADALN_K_TILE
    out = pl.pallas_call(
        _adaln_kernel,
        grid=(depth, d // tk),
        in_specs=[
            pl.BlockSpec((bp, d), lambda l, k: (0, 0)),
            pl.BlockSpec((None, tk, n), lambda l, k: (l, k, 0)),
            pl.BlockSpec((None, 1, n), lambda l, k: (l, 0, 0)),
        ],
        out_specs=pl.BlockSpec((None, bp, n), lambda l, k: (l, 0, 0)),
        out_shape=jax.ShapeDtypeStruct((depth, bp, n), F32),
        compiler_params=_params("arbitrary", "arbitrary"),
        name="adaln",
    )(cp, ada_w, ada_b.reshape(depth, 1, n))
    return out[:, :b].reshape(depth, b, N_MOD, d)


def _swiglu_tile(prep, x_rows, tm, mod_ref, g_ref, w13_ref, w2_ref, o_ref, act_ref, h_buf, row0):
    rows = FFN_ROW_BLOCK if tm % FFN_ROW_BLOCK == 0 else tm
    n_blocks = tm // rows
    n_chunks = D_FF // FF_CHUNK
    gate = 0.5 * mod_ref[row0 + 2:row0 + 3, :]

    def norm_step(k, r0):
        def run():
            h_buf[k % 2] = _norm_mod(x_rows(k, r0, rows), g_ref[...], mod_ref[row0:row0 + 1, :],
                                     mod_ref[row0 + 1:row0 + 2, :]).astype(BF16)
        return run

    steps = [list(prep(k, k * rows, rows)) + [norm_step(k, k * rows)] for k in range(n_blocks)]
    for step in steps[0]:
        step()
    for k in range(n_blocks):
        r0 = k * rows
        side = steps[k + 1] if k + 1 < n_blocks else []
        after = {((j + 1) * n_chunks) // (len(side) + 1) - 1: step for j, step in enumerate(side)}
        assert len(after) == len(side)
        h = h_buf[k % 2]
        for ci in range(n_chunks):
            lo = ci * FF_CHUNK
            a = jnp.dot(h, w13_ref[:, lo:lo + FF_CHUNK], preferred_element_type=F32)
            b = jnp.dot(h, w13_ref[:, D_FF + lo:D_FF + lo + FF_CHUNK], preferred_element_type=F32)
            act_ref[r0:r0 + rows, lo:lo + FF_CHUNK] = ((a / (1.0 + jnp.exp(-a))) * b).astype(BF16)
            if ci in after:
                after[ci]()
        y = jnp.dot(act_ref[r0:r0 + rows, :], w2_ref[...], preferred_element_type=F32)
        o_ref[r0:r0 + rows, :] = x_rows(k, r0, rows) + gate * y


def _ffn_kernel(x_ref, mod_ref, g_ref, w13_ref, w2_ref, o_ref, act_ref, h_buf):
    _swiglu_tile(lambda k, r0, rows: (), lambda k, r0, rows: x_ref[r0:r0 + rows, :], x_ref.shape[0],
                 mod_ref, g_ref, w13_ref, w2_ref, o_ref, act_ref, h_buf, 0)


def _attn_mix_ffn_kernel(x_ref, ma_ref, mb_ref, wo_ref, mod_ref, g_ref, w13_ref, w2_ref, o_ref,
                         act_ref, h_buf, x1_buf):
    half = ma_ref.shape[1]

    def prep(k, r0, rows):
        def mix():
            y = (jnp.dot(ma_ref[r0:r0 + rows, :], wo_ref[:half, :], preferred_element_type=F32)
                 + jnp.dot(mb_ref[r0:r0 + rows, :], wo_ref[half:, :], preferred_element_type=F32))
            x1_buf[k % 2] = x_ref[r0:r0 + rows, :] + mod_ref[5:6, :] * y
        return [mix]

    _swiglu_tile(prep, lambda k, r0, rows: x1_buf[k % 2], x_ref.shape[0],
                 mod_ref, g_ref, w13_ref, w2_ref, o_ref, act_ref, h_buf, 6)


def _pool_mix_ffn_kernel(x_ref, xp_ref, xn_ref, gm_ref, pw_ref, pb_ref, ps_ref, mod_ref, g_ref, w13_ref, w2_ref,
                         o_ref, act_ref, h_buf, x1_buf, *, s_len):
    tm = x_ref.shape[0]
    i = pl.program_id(1)

    def count(t, w):
        return (jnp.minimum(t + w // 2, s_len) - jnp.maximum(t - w // 2, 0)).astype(F32)

    def prep(k, r0, rows):
        if r0 == 0:
            head, head_on = xp_ref, (i > 0).astype(F32)
        else:
            head, head_on = x_ref.at[r0 - POOL_HALO:r0, :], 1.0
        if r0 + rows == tm:
            tail, tail_on = xn_ref, (i < pl.num_programs(1) - 1).astype(F32)
        else:
            tail, tail_on = x_ref.at[r0 + rows:r0 + rows + POOL_HALO, :], 1.0
        body = x_ref.at[r0:r0 + rows, :]
        inv_rms = {}

        def stats():
            for name, ref in (("head", head), ("body", body), ("tail", tail)):
                xs = ref[...]
                inv_rms[name] = lax.rsqrt(jnp.mean(xs * xs, axis=-1, keepdims=True) + EPS)

        def group_step(gi, w):
            def run():
                cols = slice(gi * POOL_GROUP_DIM, (gi + 1) * POOL_GROUP_DIM)
                gain, shift, scale = gm_ref[:, cols], mod_ref[3:4, cols], mod_ref[4:5, cols]

                def part(name, ref):
                    return ((ref[:, cols] * inv_rms[name]) * gain) * (1.0 + scale) + shift

                h = part("body", body)
                run_sum = jnp.concatenate([part("head", head) * head_on, h, part("tail", tail) * tail_on], axis=0)
                span = 1
                while span < w:
                    n = run_sum.shape[0]
                    run_sum = run_sum[:n - span] + run_sum[span:]
                    span *= 2
                start = POOL_HALO - w // 2
                seg = run_sum[start:start + rows]
                t_head = i * tm + r0 + lax.broadcasted_iota(jnp.int32, (POOL_HALO, 1), 0)
                t_tail = t_head + (rows - POOL_HALO)
                pooled = jnp.concatenate([seg[:POOL_HALO] / count(t_head, w),
                                          seg[POOL_HALO:rows - POOL_HALO] * (1.0 / w),
                                          seg[rows - POOL_HALO:] / count(t_tail, w)], axis=0)
                y = jnp.dot((pooled - h).astype(BF16), pw_ref[gi], preferred_element_type=F32)
                y = (y + pb_ref[:, cols]) * ps_ref[:, cols]
                x1_buf[k % 2, :, cols] = body[:, cols] + mod_ref[5:6, cols] * y
            return run

        return [stats] + [group_step(gi, w) for gi, w in enumerate(POOL_WINDOWS)]

    _swiglu_tile(prep, lambda k, r0, rows: x1_buf[k % 2], tm,
                 mod_ref, g_ref, w13_ref, w2_ref, o_ref, act_ref, h_buf, 6)


def _ffn_call(body, name, x, mixer_inputs, mixer_specs, mod, g, w13, w2, widx, tm):
    b, s, d = x.shape
    rows = FFN_ROW_BLOCK if tm % FFN_ROW_BLOCK == 0 else tm
    row = pl.BlockSpec((1, d), lambda bi, i: (0, 0))
    scratch = [pltpu.VMEM((tm, D_FF), BF16), pltpu.VMEM((2, rows, d), BF16)]
    if mixer_inputs:
        scratch.append(pltpu.VMEM((2, rows, d), F32))
    return pl.pallas_call(
        body,
        grid=(b, s // tm),
        in_specs=[pl.BlockSpec((None, tm, d), lambda bi, i: (bi, i, 0))] + list(mixer_specs) + [
            pl.BlockSpec((None, N_MOD, d), lambda bi, i: (bi, 0, 0)),
            row,
            _resident((None, d, 2 * D_FF), lambda bi, i: (widx, 0, 0)),
            _resident((None, D_FF, d), lambda bi, i: (widx, 0, 0)),
        ],
        out_specs=pl.BlockSpec((None, tm, d), lambda bi, i: (bi, i, 0)),
        out_shape=jax.ShapeDtypeStruct(x.shape, F32),
        scratch_shapes=scratch,
        compiler_params=_params("parallel", "parallel"),
        name=name,
    )(x, *mixer_inputs, mod, g.reshape(1, d), w13, w2)


def _ffn(x, mod, g, w13, w2, widx, tm):
    return _ffn_call(_ffn_kernel, "ffn", x, (), (), mod, g, w13, w2, widx, tm)


def _attn_mix_ffn(x, mix_a, mix_b, w_out, mod, g, w13, w2, widx, tm):
    half = mix_a.shape[-1]
    mix = pl.BlockSpec((None, tm, half), lambda bi, i: (bi, i, 0))
    specs = [mix, mix, _resident(w_out.shape, lambda bi, i: (0, 0))]
    return _ffn_call(_attn_mix_ffn_kernel, "attn_mix_ffn", x, (mix_a, mix_b, w_out), specs, mod, g, w13, w2, widx, tm)


def _pool_mix_ffn(x, g_mix, pool_w, pool_b, pool_scale, mod, g, w13, w2, widx, tm):
    b, s, d = x.shape
    per = tm // POOL_HALO
    last = s // POOL_HALO - 1
    row = pl.BlockSpec((1, d), lambda bi, i: (0, 0))
    specs = [
        pl.BlockSpec((None, POOL_HALO, d), lambda bi, i: (bi, jnp.maximum(i * per - 1, 0), 0)),
        pl.BlockSpec((None, POOL_HALO, d), lambda bi, i: (bi, jnp.minimum((i + 1) * per, last), 0)),
        row,
        pl.BlockSpec(pool_w.shape, lambda bi, i: (0, 0, 0)),
        row,
        row,
    ]
    inputs = (x, x, g_mix.reshape(1, d), pool_w, pool_b.reshape(1, d), pool_scale.reshape(1, d))
    return _ffn_call(functools.partial(_pool_mix_ffn_kernel, s_len=s), "pool_mix_ffn", x, inputs, specs,
                     mod, g, w13, w2, widx, tm)


def _proj_kernel(x_ref, mod_ref, g_ref, wt_ref, gain_ref, cos_ref, sin_ref,
                 qa_ref, ka_ref, va_ref, qb_ref, kb_ref, vb_ref, h_buf):
    tm = x_ref.shape[0]
    ts = PROJ_TOKEN_SUB if tm % PROJ_TOKEN_SUB == 0 else tm
    width = B_HEADS * HEAD_DIM

    def norm_step(j):
        h_buf[j % 2] = _norm_mod(x_ref[j * ts:(j + 1) * ts, :], g_ref[...], mod_ref[3:4, :],
                                 mod_ref[4:5, :]).astype(BF16)

    norm_step(0)
    for j in range(tm // ts):
        tok = slice(j * ts, (j + 1) * ts)
        h = h_buf[j % 2]
        cos = cos_ref[:, tok]
        sin = sin_ref[:, tok]

        def group_t(gi):
            return lax.dot_general(wt_ref[gi * width:(gi + 1) * width, :], h,
                                   (((1,), (1,)), ((), ())), preferred_element_type=F32)

        def head_norm(ut, gain):
            x3 = ut.reshape(B_HEADS, HEAD_DIM, ts)
            ms = jnp.mean(x3 * x3, axis=1, keepdims=True)
            return (x3 * lax.rsqrt(ms + EPS)) * gain[None]

        def rope(x3):
            x1 = x3[:, :HEAD_DIM // 2, :]
            x2 = x3[:, HEAD_DIM // 2:, :]
            return jnp.concatenate([x1 * cos - x2 * sin, x1 * sin + x2 * cos], axis=1)

        qa = rope(head_norm(group_t(0), gain_ref[0, :, tok])) * QK_SCALE
        qa_ref[:, :, tok] = qa.reshape(A_HEADS, A_VDIM, ts).astype(BF16)
        ka = rope(head_norm(group_t(1), gain_ref[1, :, tok])).reshape(A_HEADS, A_VDIM, ts)
        for hh in range(A_HEADS):
            ka_ref[hh, tok, :] = ka[hh].T.astype(BF16)
        if (j + 1) * ts < tm:
            norm_step(j + 1)
        qb = head_norm(group_t(3), gain_ref[2, :, tok]) * QK_SCALE
        qb_ref[:, :, tok] = qb.astype(BF16)
        kb = head_norm(group_t(4), gain_ref[3, :, tok]).reshape(B_HEADS // 2, 2 * HEAD_DIM, ts)
        for hp in range(B_HEADS // 2):
            kb_ref[hp, tok, :] = kb[hp].T.astype(BF16)
        va_ref[:, :A_VDIM, tok] = group_t(2).reshape(A_HEADS, A_VDIM, ts).astype(BF16)
        va_ref[:, A_VDIM:, tok] = jnp.ones((A_HEADS, DA_VROWS - A_VDIM, ts), BF16)
        vb_ref[:, :HEAD_DIM, tok] = group_t(5).reshape(B_HEADS, HEAD_DIM, ts).astype(BF16)
        vb_ref[:, HEAD_DIM:, tok] = jnp.ones((B_HEADS, NA_VROWS - HEAD_DIM, ts), BF16)


def _proj(x, mod, g, w_in_t, gains, cos_t, sin_t, tm):
    b, s, d = x.shape
    tok = lambda bi, i: (bi, 0, 0, i)
    seq = lambda bi, i: (bi, 0, i, 0)
    return pl.pallas_call(
        _proj_kernel,
        grid=(b, s // tm),
        in_specs=[
            pl.BlockSpec((None, tm, d), lambda bi, i: (bi, i, 0)),
            pl.BlockSpec((None, N_MOD, d), lambda bi, i: (bi, 0, 0)),
            pl.BlockSpec((1, d), lambda bi, i: (0, 0)),
            _resident(w_in_t.shape, lambda bi, i: (0, 0)),
            pl.BlockSpec((4, HEAD_DIM, tm), lambda bi, i: (0, 0, 0)),
            pl.BlockSpec((HEAD_DIM // 2, tm), lambda bi, i: (0, i)),
            pl.BlockSpec((HEAD_DIM // 2, tm), lambda bi, i: (0, i)),
        ],
        out_specs=[
            pl.BlockSpec((None, A_HEADS, A_VDIM, tm), tok),
            pl.BlockSpec((None, A_HEADS, tm, A_VDIM), seq),
            pl.BlockSpec((None, A_HEADS, DA_VROWS, tm), tok),
            pl.BlockSpec((None, B_HEADS, HEAD_DIM, tm), tok),
            pl.BlockSpec((None, B_HEADS // 2, tm, 2 * HEAD_DIM), seq),
            pl.BlockSpec((None, B_HEADS, NA_VROWS, tm), tok),
        ],
        out_shape=[
            jax.ShapeDtypeStruct((b, A_HEADS, A_VDIM, s), BF16),
            jax.ShapeDtypeStruct((b, A_HEADS, s, A_VDIM), BF16),
            jax.ShapeDtypeStruct((b, A_HEADS, DA_VROWS, s), BF16),
            jax.ShapeDtypeStruct((b, B_HEADS, HEAD_DIM, s), BF16),
            jax.ShapeDtypeStruct((b, B_HEADS // 2, s, 2 * HEAD_DIM), BF16),
            jax.ShapeDtypeStruct((b, B_HEADS, NA_VROWS, s), BF16),
        ],
        scratch_shapes=[pltpu.VMEM((2, PROJ_TOKEN_SUB if tm % PROJ_TOKEN_SUB == 0 else tm, d), BF16)],
        compiler_params=_params("parallel", "parallel"),
        name="qkv_proj",
    )(x, mod, g.reshape(1, d), w_in_t, gains, cos_t, sin_t)


def _dattn_kernel(q_ref, k_ref, v_ref, lam_ref, gn_ref, sg_ref, w13f_ref, w2f_ref, o_ref, w13b_ref, w2b_ref,
                  acc_ref, *, kblk, lambda_init):
    s_len = k_ref.shape[0]
    n_strips = q_ref.shape[1] // DA_QSUB
    lp = lam_ref[...]
    lam = (jnp.exp(jnp.sum(lp[0:1] * lp[1:2], axis=-1, keepdims=True))
           - jnp.exp(jnp.sum(lp[2:3] * lp[3:4], axis=-1, keepdims=True)) + lambda_init)
    gmax = jnp.max(jnp.abs(gn_ref[...]), axis=-1, keepdims=True)
    shift = (DA_BOUND_MARGIN * HEAD_DIM * QK_SCALE) * gmax[0:1] * gmax[1:2]
    z = jnp.zeros((HEAD_DIM, DA_QSUB), BF16)

    def strip_rhs(sb):
        qt = q_ref[:, sb * DA_QSUB:(sb + 1) * DA_QSUB]
        return jnp.concatenate([jnp.concatenate([qt[:HEAD_DIM], z], axis=0),
                                jnp.concatenate([z, qt[HEAD_DIM:]], axis=0)], axis=1)

    def finish(sb, num, den):
        o = num / den
        ot = o[:, :DA_QSUB] - lam * o[:, DA_QSUB:]
        ms = jnp.mean(ot * ot, axis=0, keepdims=True)
        y = ((ot * lax.rsqrt(ms + EPS)) * sg_ref[...]) * (1.0 - lambda_init)
        o_ref[sb * DA_QSUB:(sb + 1) * DA_QSUB, :] = y.T.astype(BF16)

    min_sum = None
    for sb in range(n_strips):
        if sb == n_strips // 2:
            w13b_ref[...] = w13f_ref[...].astype(BF16)
            w2b_ref[...] = w2f_ref[...].astype(BF16)
        rhs = strip_rhs(sb)
        num = None
        den8 = None
        for k0 in range(0, s_len, kblk):
            st = jnp.dot(k_ref[k0:k0 + kblk, :], rhs, preferred_element_type=F32)
            p = jnp.exp(st - shift)
            part = jnp.sum(p.reshape(kblk // 8, 8, 2 * DA_QSUB), axis=0)
            pv = jnp.dot(v_ref[:A_VDIM, k0:k0 + kblk], p.astype(BF16), preferred_element_type=F32)
            num = pv if num is None else num + pv
            den8 = part if den8 is None else den8 + part
        den = jnp.sum(den8, axis=0, keepdims=True)
        finish(sb, num, den)
        min_sum = den if min_sum is None else jnp.minimum(min_sum, den)

    sums_ok = jnp.min(min_sum) >= DA_MIN_SUM

    @pl.when(jnp.logical_not(sums_ok))
    def _():
        for sb in range(n_strips):
            rhs = strip_rhs(sb)
            acc_ref[...] = jnp.zeros_like(acc_ref)

            def safe_body(t, m):
                k0 = pl.multiple_of(t * DA_SAFE_KBLK, DA_SAFE_KBLK)
                st = jnp.dot(k_ref[pl.ds(k0, DA_SAFE_KBLK), :], rhs, preferred_element_type=F32)
                m_new = jnp.maximum(m, jnp.max(st, axis=0, keepdims=True))
                pv = jnp.dot(v_ref[:, pl.ds(k0, DA_SAFE_KBLK)], jnp.exp(st - m_new).astype(BF16),
                             preferred_element_type=F32)
                acc_ref[...] = acc_ref[...] * jnp.exp(m - m_new) + pv
                return m_new

            lax.fori_loop(0, s_len // DA_SAFE_KBLK, safe_body, jnp.full((1, 2 * DA_QSUB), NEG_INF, F32))
            finish(sb, acc_ref[:A_VDIM, :], acc_ref[A_VDIM:A_VDIM + 1, :])


def _cast_rows(total, steps):
    return min(r for r in range(BF16_SUBLANES, total + 1, BF16_SUBLANES) if total % r == 0 and total // r <= steps)


def _dattn(qa_t, ka, va_t, a_lambda, a_qk_gain, a_subln, lambda_init, qblk, ff_w13, ff_w2):
    b, nh, _, s = qa_t.shape
    assert qblk % DA_QSUB == 0
    nq = s // qblk
    depth, halves, d, f2 = ff_w13.shape
    n_sets = depth * halves - 1
    per_set = (b * nh * nq) // n_sets
    assert per_set >= 1
    rows13, rows2 = _cast_rows(d, per_set), _cast_rows(f2 // 2, per_set)

    def cast_block(rows_total, rows):
        nblk = rows_total // rows

        def locate(bi, h, i):
            blk = jnp.minimum((bi * nh + h) * nq + i, n_sets * nblk - 1)
            return blk // nblk, blk % nblk

        def src(bi, h, i):
            st, rb = locate(bi, h, i)
            return ((st + 1) // halves, (st + 1) % halves, rb, 0)

        def dst(bi, h, i):
            st, rb = locate(bi, h, i)
            return (st, rb, 0)

        return src, dst

    src13, dst13 = cast_block(d, rows13)
    src2, dst2 = cast_block(f2 // 2, rows2)
    return pl.pallas_call(
        functools.partial(_dattn_kernel, kblk=_tile(s, DA_KBLK), lambda_init=lambda_init),
        grid=(b, nh, nq),
        in_specs=[
            pl.BlockSpec((None, None, A_VDIM, qblk), lambda bi, h, i: (bi, h, 0, i)),
            pl.BlockSpec((None, None, s, A_VDIM), lambda bi, h, i: (bi, h, 0, 0)),
            pl.BlockSpec((None, None, DA_VROWS, s), lambda bi, h, i: (bi, h, 0, 0)),
            pl.BlockSpec((4, HEAD_DIM), lambda bi, h, i: (0, 0)),
            pl.BlockSpec((2, HEAD_DIM), lambda bi, h, i: (0, 0)),
            pl.BlockSpec((A_VDIM, DA_QSUB), lambda bi, h, i: (0, 0)),
            pl.BlockSpec((None, None, rows13, f2), src13),
            pl.BlockSpec((None, None, rows2, d), src2),
        ],
        out_specs=[
            pl.BlockSpec((None, qblk, A_VDIM), lambda bi, h, i: (bi, i, h)),
            pl.BlockSpec((None, rows13, f2), dst13),
            pl.BlockSpec((None, rows2, d), dst2),
        ],
        out_shape=[
            jax.ShapeDtypeStruct((b, s, nh * A_VDIM), BF16),
            jax.ShapeDtypeStruct((n_sets, d, f2), BF16),
            jax.ShapeDtypeStruct((n_sets, f2 // 2, d), BF16),
        ],
        scratch_shapes=[pltpu.VMEM((DA_VROWS, 2 * DA_QSUB), F32)],
        compiler_params=_params("arbitrary", "arbitrary", "arbitrary"),
        name="diff_attn",
    )(qa_t, ka, va_t, a_lambda, a_qk_gain, jnp.broadcast_to(a_subln[:, None], (A_VDIM, DA_QSUB)), ff_w13, ff_w2)


def _toeplitz_selector():
    kc = np.arange(GRID_W)[:, None]
    qc = np.arange(GRID_W)[None, :]
    cs = np.clip(qc - NA_COLS // 2, 0, GRID_W - NA_COLS)
    col_ok = (kc >= cs) & (kc < cs + NA_COLS)
    dc = np.clip(kc - qc + NA_COLS - 1, 0, 2 * NA_COLS - 2)
    sel = (np.arange(2 * NA_COLS)[:, None, None] == dc[None]) & col_ok[None]
    mask = np.where(col_ok, 0.0, NEG_INF)
    return (sel.reshape(2 * NA_COLS, GRID_W * GRID_W).astype(np.float32),
            mask.reshape(1, GRID_W * GRID_W).astype(np.float32))


def _rpb_expand_kernel(r_ref, gn_ref, sel_ref, mask_ref, o_ref, *, ndr, ndc):
    r = r_ref[...]
    nh = r.shape[0] // NA_DR_PAD
    rows = lax.broadcasted_iota(jnp.int32, r.shape, 0)
    cols = lax.broadcasted_iota(jnp.int32, r.shape, 1)
    real = ((rows & (NA_DR_PAD - 1)) < ndr) & (cols < ndc)
    row_max = jnp.max(jnp.where(real, r, NEG_INF), axis=1, keepdims=True)
    head_max = jnp.max(row_max.reshape(nh, NA_DR_PAD, 1), axis=1, keepdims=True)
    gmax = jnp.max(jnp.abs(gn_ref[...]), axis=-1, keepdims=True)
    bound = (DA_BOUND_MARGIN * HEAD_DIM * QK_SCALE) * gmax[0:1] * gmax[1:2]
    shift = jnp.broadcast_to(head_max + bound[None], (nh, NA_DR_PAD, 1)).reshape(r.shape[0], 1)
    o_ref[...] = (jnp.dot(r, sel_ref[...], preferred_element_type=F32, precision=lax.Precision.HIGHEST)
                  + mask_ref[...]) - shift


def _na_bias_table(rpb, qk_gain):
    nh, ndr, ndc = rpb.shape
    sel, mask = _toeplitz_selector()
    r = jnp.pad(rpb, ((0, 0), (0, NA_DR_PAD - ndr), (0, sel.shape[0] - ndc))).reshape(nh * NA_DR_PAD, sel.shape[0])
    flat = pl.pallas_call(
        functools.partial(_rpb_expand_kernel, ndr=ndr, ndc=ndc),
        out_shape=jax.ShapeDtypeStruct((nh * NA_DR_PAD, GRID_W * GRID_W), F32),
        name="rpb_expand",
    )(r, qk_gain, sel, mask)
    tile = flat.reshape(nh, NA_DR_PAD, GRID_W, GRID_W)
    neg = jnp.full((nh, GRID_W, GRID_W), NEG_INF, F32)
    half = NA_ROWS // 2
    kinds = []
    for kind in range(3):
        win_rows = []
        for i in range(NA_WIN_ROWS):
            blocks = []
            for j in range(NA_BLOCK_ROWS):
                if kind == 0:
                    dr, ok = i - j + NA_ROWS - 1, i < NA_ROWS
                elif kind == 1:
                    dr, ok = i - j + half - 1, j <= i < j + NA_ROWS
                else:
                    dr, ok = i - j - 1, i >= NA_WIN_ROWS - NA_ROWS
                blocks.append(tile[:, dr] if ok else neg)
            win_rows.append(jnp.concatenate(blocks, axis=-1))
        kinds.append(jnp.concatenate(win_rows, axis=1))
    return jnp.stack(kinds)


def _na_kernel(q_ref, k_ref, v_ref, bias_ref, o_ref, *, nblk):
    qn = NA_BLOCK_ROWS * GRID_W
    win = NA_WIN_ROWS * GRID_W
    nh = q_ref.shape[0]
    per_step = q_ref.shape[2] // qn
    z = jnp.zeros((HEAD_DIM, qn), BF16)

    def block(j):
        rb = pl.program_id(1) * per_step + j
        t0 = pl.multiple_of(jnp.clip(rb - 1, 0, nblk - 3) * qn, qn)
        kind = jnp.where(rb == 0, 0, jnp.where(rb == nblk - 1, 2, 1))

        def logits(h):
            qt = q_ref[h, :, j * qn:(j + 1) * qn]
            rhs = jnp.concatenate([qt, z] if h % 2 == 0 else [z, qt], axis=0)
            kwin = k_ref[h // 2, pl.ds(t0, win), :]
            return jnp.dot(kwin, rhs, preferred_element_type=F32) + bias_ref[kind, h]

        def weighted(h, p):
            return jnp.dot(v_ref[h, :, pl.ds(t0, win)], p.astype(BF16), preferred_element_type=F32)

        return logits, weighted

    def store(j, pvs):
        outs = [pv[:HEAD_DIM] / pv[HEAD_DIM:HEAD_DIM + 1] for pv in pvs]
        o_ref[j * qn:(j + 1) * qn, :] = jnp.concatenate(outs, axis=0).T.astype(BF16)

    sums = []
    for j in range(per_step):
        logits, weighted = block(j)
        sts = [logits(h) for h in range(nh)]
        pvs = [weighted(h, jnp.exp(st)) for h, st in enumerate(sts)]
        store(j, pvs)
        sums += [pv[HEAD_DIM:HEAD_DIM + 1] for pv in pvs]

    sums_ok = jnp.min(functools.reduce(jnp.minimum, sums)) >= NA_MIN_SUM

    @pl.when(jnp.logical_not(sums_ok))
    def _():
        for j in range(per_step):
            logits, weighted = block(j)
            safe = []
            for h in range(nh):
                st = logits(h)
                safe.append(weighted(h, jnp.exp(st - jnp.max(st, axis=0, keepdims=True))))
            store(j, safe)


def _na(qb_t, kb, vb_t, bias_tab):
    b, nh, _, s = qb_t.shape
    qn = NA_BLOCK_ROWS * GRID_W
    nblk = s // qn
    assert nblk >= 3
    per_step = NA_STEP_BLOCKS if nblk % NA_STEP_BLOCKS == 0 else 1
    qn *= per_step
    return pl.pallas_call(
        functools.partial(_na_kernel, nblk=nblk),
        grid=(b, nblk // per_step),
        in_specs=[
            pl.BlockSpec((None, nh, HEAD_DIM, qn), lambda bi, rb: (bi, 0, 0, rb)),
            _resident((None, nh // 2, s, 2 * HEAD_DIM), lambda bi, rb: (bi, 0, 0, 0)),
            _resident((None, nh, NA_VROWS, s), lambda bi, rb: (bi, 0, 0, 0)),
            _resident(bias_tab.shape, lambda bi, rb: (0, 0, 0, 0)),
        ],
        out_specs=pl.BlockSpec((None, qn, nh * HEAD_DIM), lambda bi, rb: (bi, rb, 0)),
        out_shape=jax.ShapeDtypeStruct((b, s, nh * HEAD_DIM), BF16),
        compiler_params=_params("parallel", "arbitrary"),
        name="nbr_attn",
    )(qb_t, kb, vb_t, bias_tab)


def _tile(s, want):
    return want if s % want == 0 else s


def kernel(x, c, ada_w, ada_b, norm_g, ff_w13, ff_w2, w_in, w_out, a_qk_norm, a_lambda, a_subln,
           b_qk_norm, b_rpb, pool_w, pool_b, pool_scale):
    b, s, d = x.shape
    depth = ada_w.shape[0]
    tm = _tile(s, TOKEN_TILE)
    ffn_tm = _tile(s, FFN_TOKEN_TILE)
    mod = _adaln(c, ada_w, ada_b)

    pos = jnp.arange(s, dtype=F32)
    inv = ROPE_THETA ** (-jnp.arange(0, HEAD_DIM, 2, dtype=F32) / HEAD_DIM)
    ang = inv[:, None] * pos[None, :]
    cos_t, sin_t = jnp.cos(ang), jnp.sin(ang)

    first = (ff_w13[0, :1].astype(BF16), ff_w2[0, :1].astype(BF16))
    rest = None

    def ffn_weights(layer, half):
        idx = 2 * layer + half
        return (*first, 0) if idx == 0 else (*rest, idx - 1)

    for layer in range(depth):
        m = mod[layer]
        x = _ffn(x, m, norm_g[layer, 0], *ffn_weights(layer, 0), ffn_tm)
        if layer % 2 == 0:
            e = layer // 2
            lambda_init = 0.8 - 0.6 * math.exp(-0.3 * layer)
            gains = jnp.stack([a_qk_norm[e, 0], a_qk_norm[e, 1], b_qk_norm[e, 0], b_qk_norm[e, 1]])
            gains = jnp.broadcast_to(gains[:, :, None], (4, HEAD_DIM, tm))
            qa_t, ka, va_t, qb_t, kb, vb_t = _proj(x, m, norm_g[layer, 1], w_in[e].T.astype(BF16),
                                                   gains, cos_t, sin_t, tm)
            mix_a, w13_rest, w2_rest = _dattn(qa_t, ka, va_t, a_lambda[e], a_qk_norm[e], a_subln[e], lambda_init,
                                              _tile(s, DA_QBLK), ff_w13, ff_w2)
            if rest is None:
                rest = (w13_rest, w2_rest)
            mix_b = _na(qb_t, kb, vb_t, _na_bias_table(b_rpb[e], b_qk_norm[e]))
            x = _attn_mix_ffn(x, mix_a, mix_b, w_out[e].astype(BF16), m, norm_g[layer, 2],
                              *ffn_weights(layer, 1), ffn_tm)
        else:
            o = layer // 2
            x = _pool_mix_ffn(x, norm_g[layer, 1], pool_w[o].astype(BF16), pool_b[o], pool_scale[o],
                              m, norm_g[layer, 2], *ffn_weights(layer, 1), ffn_tm)
    return x
```

```python
import functools
import math

import numpy as np
import jax
import jax.numpy as jnp
from jax import lax
from jax.experimental import pallas as pl
from jax.experimental.pallas import tpu as pltpu

F32 = jnp.float32
BF16 = jnp.bfloat16

D_MODEL = 1024
HEAD_DIM = 64
A_HEADS = 4
A_VDIM = 2 * HEAD_DIM
B_HEADS = 8
D_FF = 2816
GRID_W = 64
NA_ROWS = 8
NA_COLS = 16
POOL_WINDOWS = (2, 4, 8, 16)
POOL_GROUP_DIM = D_MODEL // len(POOL_WINDOWS)
ROPE_THETA = 10000.0
EPS = 1e-6
N_MOD = 9
NEG_INF = -1e30
QK_SCALE = HEAD_DIM ** -0.5

V7X_VMEM_LIMIT_BYTES = 56 * 1024 * 1024
BF16_SUBLANES = 16
ADALN_K_TILE = 256
TOKEN_TILE = 1024
PROJ_TOKEN_SUB = 256
FFN_TOKEN_TILE = 1024
FFN_ROW_BLOCK = 256
FF_CHUNK = 256
DA_QBLK = 1024
DA_QSUB = 512
DA_KBLK = 4096
DA_BOUND_MARGIN = 1.02
DA_MIN_SUM = 1e-18
DA_SAFE_KBLK = 256
DA_VROWS = A_VDIM + BF16_SUBLANES
NA_BLOCK_ROWS = 4
NA_WIN_ROWS = 3 * NA_BLOCK_ROWS
NA_STEP_BLOCKS = 4
NA_HEAD_GROUPS = 2
NA_VROWS = HEAD_DIM + BF16_SUBLANES
NA_DR_PAD = 16
NA_MIN_SUM = 1e-18
POOL_HALO = 8


def _params(*sem):
    return pltpu.CompilerParams(dimension_semantics=sem, vmem_limit_bytes=V7X_VMEM_LIMIT_BYTES)


def _resident(shape, index_map):
    return pl.BlockSpec(shape, index_map, pipeline_mode=pl.Buffered(1))


def _norm_mod(x, g, shift, scale):
    ms = jnp.mean(x * x, axis=-1, keepdims=True)
    y = (x * lax.rsqrt(ms + EPS)) * g
    return y * (1.0 + scale) + shift


def _adaln_kernel(c_ref, w_ref, b_ref, o_ref):
    tk = w_ref.shape[0]
    k = pl.program_id(1)
    c = c_ref[:, pl.ds(pl.multiple_of(k * tk, tk), tk)]
    cond = c / (1.0 + jnp.exp(-c))

    @pl.when(k == 0)
    def _():
        o_ref[...] = jnp.broadcast_to(b_ref[...], o_ref.shape)

    o_ref[...] += jnp.dot(cond.astype(BF16), w_ref[...].astype(BF16), preferred_element_type=F32)


def _adaln(c, ada_w, ada_b):
    depth, d, n = ada_w.shape
    b = c.shape[0]
    bp = -(-b // 8) * 8
    cp = jnp.pad(c, ((0, bp - b), (0, 0)))
    tk = ADALN_K_TILE
    out = pl.pallas_call(
        _adaln_kernel,
        grid=(depth, d // tk),
        in_specs=[
            pl.BlockSpec((bp, d), lambda l, k: (0, 0)),
            pl.BlockSpec((None, tk, n), lambda l, k: (l, k, 0)),
            pl.BlockSpec((None, 1, n), lambda l, k: (l, 0, 0)),
        ],
        out_specs=pl.BlockSpec((None, bp, n), lambda l, k: (l, 0, 0)),
        out_shape=jax.ShapeDtypeStruct((depth, bp, n), F32),
        compiler_params=_params("arbitrary", "arbitrary"),
        name="adaln",
    )(cp, ada_w, ada_b.reshape(depth, 1, n))
    return out[:, :b].reshape(depth, b, N_MOD, d)


def _swiglu_tile(prep, x_rows, tm, mod_ref, g_ref, w13_ref, w2_ref, o_ref, act_ref, h_buf, row0):
    rows = FFN_ROW_BLOCK if tm % FFN_ROW_BLOCK == 0 else tm
    n_blocks = tm // rows
    n_chunks = D_FF // FF_CHUNK
    gate = 0.5 * mod_ref[row0 + 2:row0 + 3, :]

    def norm_step(k, r0):
        def run():
            h_buf[k % 2] = _norm_mod(x_rows(k, r0, rows), g_ref[...], mod_ref[row0:row0 + 1, :],
                                     mod_ref[row0 + 1:row0 + 2, :]).astype(BF16)
        return run

    steps = [list(prep(k, k * rows, rows)) + [norm_step(k, k * rows)] for k in range(n_blocks)]
    for step in steps[0]:
        step()
    for k in range(n_blocks):
        r0 = k * rows
        side = steps[k + 1] if k + 1 < n_blocks else []
        after = {((j + 1) * n_chunks) // (len(side) + 1) - 1: step for j, step in enumerate(side)}
        assert len(after) == len(side)
        h = h_buf[k % 2]
        for ci in range(n_chunks):
            lo = ci * FF_CHUNK
            a = jnp.dot(h, w13_ref[:, lo:lo + FF_CHUNK], preferred_element_type=F32)
            b = jnp.dot(h, w13_ref[:, D_FF + lo:D_FF + lo + FF_CHUNK], preferred_element_type=F32)
            act_ref[r0:r0 + rows, lo:lo + FF_CHUNK] = ((a / (1.0 + jnp.exp(-a))) * b).astype(BF16)
            if ci in after:
                after[ci]()
        y = jnp.dot(act_ref[r0:r0 + rows, :], w2_ref[...], preferred_element_type=F32)
        o_ref[r0:r0 + rows, :] = x_rows(k, r0, rows) + gate * y


def _ffn_kernel(x_ref, mod_ref, g_ref, w13_ref, w2_ref, o_ref, act_ref, h_buf):
    _swiglu_tile(lambda k, r0, rows: (), lambda k, r0, rows: x_ref[r0:r0 + rows, :], x_ref.shape[0],
                 mod_ref, g_ref, w13_ref, w2_ref, o_ref, act_ref, h_buf, 0)


def _attn_mix_ffn_kernel(x_ref, ma_ref, mb_ref, wo_ref, mod_ref, g_ref, w13_ref, w2_ref, o_ref,
                         act_ref, h_buf, x1_buf):
    half = ma_ref.shape[1]

    def prep(k, r0, rows):
        def mix():
            y = (jnp.dot(ma_ref[r0:r0 + rows, :], wo_ref[:half, :], preferred_element_type=F32)
                 + jnp.dot(mb_ref[r0:r0 + rows, :], wo_ref[half:, :], preferred_element_type=F32))
            x1_buf[k % 2] = x_ref[r0:r0 + rows, :] + mod_ref[5:6, :] * y
        return [mix]

    _swiglu_tile(prep, lambda k, r0, rows: x1_buf[k % 2], x_ref.shape[0],
                 mod_ref, g_ref, w13_ref, w2_ref, o_ref, act_ref, h_buf, 6)


def _pool_mix_ffn_kernel(x_ref, xp_ref, xn_ref, gm_ref, pw_ref, pb_ref, ps_ref, mod_ref, g_ref, w13_ref, w2_ref,
                         o_ref, act_ref, h_buf, x1_buf, *, s_len):
    tm = x_ref.shape[0]
    i = pl.program_id(1)

    def count(t, w):
        return (jnp.minimum(t + w // 2, s_len) - jnp.maximum(t - w // 2, 0)).astype(F32)

    def prep(k, r0, rows):
        if r0 == 0:
            head, head_on = xp_ref, (i > 0).astype(F32)
        else:
            head, head_on = x_ref.at[r0 - POOL_HALO:r0, :], 1.0
        if r0 + rows == tm:
            tail, tail_on = xn_ref, (i < pl.num_programs(1) - 1).astype(F32)
        else:
            tail, tail_on = x_ref.at[r0 + rows:r0 + rows + POOL_HALO, :], 1.0
        body = x_ref.at[r0:r0 + rows, :]
        inv_rms = {}

        def stats():
            for name, ref in (("head", head), ("body", body), ("tail", tail)):
                xs = ref[...]
                inv_rms[name] = lax.rsqrt(jnp.mean(xs * xs, axis=-1, keepdims=True) + EPS)

        def group_step(gi, w):
            def run():
                cols = slice(gi * POOL_GROUP_DIM, (gi + 1) * POOL_GROUP_DIM)
                gain, shift, scale = gm_ref[:, cols], mod_ref[3:4, cols], mod_ref[4:5, cols]

                def part(name, ref):
                    return ((ref[:, cols] * inv_rms[name]) * gain) * (1.0 + scale) + shift

                h = part("body", body)
                run_sum = jnp.concatenate([part("head", head) * head_on, h, part("tail", tail) * tail_on], axis=0)
                span = 1
                while span < w:
                    n = run_sum.shape[0]
                    run_sum = run_sum[:n - span] + run_sum[span:]
                    span *= 2
                start = POOL_HALO - w // 2
                seg = run_sum[start:start + rows]
                t_head = i * tm + r0 + lax.broadcasted_iota(jnp.int32, (POOL_HALO, 1), 0)
                t_tail = t_head + (rows - POOL_HALO)
                pooled = jnp.concatenate([seg[:POOL_HALO] / count(t_head, w),
                                          seg[POOL_HALO:rows - POOL_HALO] * (1.0 / w),
                                          seg[rows - POOL_HALO:] / count(t_tail, w)], axis=0)
                y = jnp.dot((pooled - h).astype(BF16), pw_ref[gi], preferred_element_type=F32)
                y = (y + pb_ref[:, cols]) * ps_ref[:, cols]
                x1_buf[k % 2, :, cols] = body[:, cols] + mod_ref[5:6, cols] * y
            return run

        return [stats] + [group_step(gi, w) for gi, w in enumerate(POOL_WINDOWS)]

    _swiglu_tile(prep, lambda k, r0, rows: x1_buf[k % 2], tm,
                 mod_ref, g_ref, w13_ref, w2_ref, o_ref, act_ref, h_buf, 6)


def _ffn_call(body, name, x, mixer_inputs, mixer_specs, mod, g, w13, w2, widx, tm):
    b, s, d = x.shape
    rows = FFN_ROW_BLOCK if tm % FFN_ROW_BLOCK == 0 else tm
    row = pl.BlockSpec((1, d), lambda bi, i: (0, 0))
    scratch = [pltpu.VMEM((tm, D_FF), BF16), pltpu.VMEM((2, rows, d), BF16)]
    if mixer_inputs:
        scratch.append(pltpu.VMEM((2, rows, d), F32))
    return pl.pallas_call(
        body,
        grid=(b, s // tm),
        in_specs=[pl.BlockSpec((None, tm, d), lambda bi, i: (bi, i, 0))] + list(mixer_specs) + [
            pl.BlockSpec((None, N_MOD, d), lambda bi, i: (bi, 0, 0)),
            row,
            _resident((None, d, 2 * D_FF), lambda bi, i: (widx, 0, 0)),
            _resident((None, D_FF, d), lambda bi, i: (widx, 0, 0)),
        ],
        out_specs=pl.BlockSpec((None, tm, d), lambda bi, i: (bi, i, 0)),
        out_shape=jax.ShapeDtypeStruct(x.shape, F32),
        scratch_shapes=scratch,
        compiler_params=_params("parallel", "parallel"),
        name=name,
    )(x, *mixer_inputs, mod, g.reshape(1, d), w13, w2)


def _ffn(x, mod, g, w13, w2, widx, tm):
    return _ffn_call(_ffn_kernel, "ffn", x, (), (), mod, g, w13, w2, widx, tm)


def _attn_mix_ffn(x, mix_a, mix_b, w_out, mod, g, w13, w2, widx, tm):
    half = mix_a.shape[-1]
    mix = pl.BlockSpec((None, tm, half), lambda bi, i: (bi, i, 0))
    specs = [mix, mix, _resident(w_out.shape, lambda bi, i: (0, 0))]
    return _ffn_call(_attn_mix_ffn_kernel, "attn_mix_ffn", x, (mix_a, mix_b, w_out), specs, mod, g, w13, w2, widx, tm)


def _pool_mix_ffn(x, g_mix, pool_w, pool_b, pool_scale, mod, g, w13, w2, widx, tm):
    b, s, d = x.shape
    per = tm // POOL_HALO
    last = s // POOL_HALO - 1
    row = pl.BlockSpec((1, d), lambda bi, i: (0, 0))
    specs = [
        pl.BlockSpec((None, POOL_HALO, d), lambda bi, i: (bi, jnp.maximum(i * per - 1, 0), 0)),
        pl.BlockSpec((None, POOL_HALO, d), lambda bi, i: (bi, jnp.minimum((i + 1) * per, last), 0)),
        row,
        pl.BlockSpec(pool_w.shape, lambda bi, i: (0, 0, 0)),
        row,
        row,
    ]
    inputs = (x, x, g_mix.reshape(1, d), pool_w, pool_b.reshape(1, d), pool_scale.reshape(1, d))
    return _ffn_call(functools.partial(_pool_mix_ffn_kernel, s_len=s), "pool_mix_ffn", x, inputs, specs,
                     mod, g, w13, w2, widx, tm)


def _proj_kernel(x_ref, mod_ref, g_ref, wt_ref, gain_ref, cos_ref, sin_ref,
                 qa_ref, ka_ref, va_ref, qb_ref, kb_ref, vb_ref, h_buf):
    tm = x_ref.shape[0]
    ts = PROJ_TOKEN_SUB if tm % PROJ_TOKEN_SUB == 0 else tm
    width = B_HEADS * HEAD_DIM

    def norm_step(j):
        h_buf[j % 2] = _norm_mod(x_ref[j * ts:(j + 1) * ts, :], g_ref[...], mod_ref[3:4, :],
                                 mod_ref[4:5, :]).astype(BF16)

    norm_step(0)
    for j in range(tm // ts):
        tok = slice(j * ts, (j + 1) * ts)
        h = h_buf[j % 2]
        cos = cos_ref[:, tok]
        sin = sin_ref[:, tok]

        def group_t(gi):
            return lax.dot_general(wt_ref[gi * width:(gi + 1) * width, :], h,
                                   (((1,), (1,)), ((), ())), preferred_element_type=F32)

        def head_norm(ut, gain):
            x3 = ut.reshape(B_HEADS, HEAD_DIM, ts)
            ms = jnp.mean(x3 * x3, axis=1, keepdims=True)
            return (x3 * lax.rsqrt(ms + EPS)) * gain[None]

        def rope(x3):
            x1 = x3[:, :HEAD_DIM // 2, :]
            x2 = x3[:, HEAD_DIM // 2:, :]
            return jnp.concatenate([x1 * cos - x2 * sin, x1 * sin + x2 * cos], axis=1)

        qa = rope(head_norm(group_t(0), gain_ref[0, :, tok])) * QK_SCALE
        qa_ref[:, :, tok] = qa.reshape(A_HEADS, A_VDIM, ts).astype(BF16)
        ka = rope(head_norm(group_t(1), gain_ref[1, :, tok])).reshape(A_HEADS, A_VDIM, ts)
        for hh in range(A_HEADS):
            ka_ref[hh, tok, :] = ka[hh].T.astype(BF16)
        if (j + 1) * ts < tm:
            norm_step(j + 1)
        qb = head_norm(group_t(3), gain_ref[2, :, tok]) * QK_SCALE
        qb_ref[:, :, tok] = qb.astype(BF16)
        kb = head_norm(group_t(4), gain_ref[3, :, tok]).reshape(B_HEADS // 2, 2 * HEAD_DIM, ts)
        for hp in range(B_HEADS // 2):
            kb_ref[hp, tok, :] = kb[hp].T.astype(BF16)
        va_ref[:, :A_VDIM, tok] = group_t(2).reshape(A_HEADS, A_VDIM, ts).astype(BF16)
        va_ref[:, A_VDIM:, tok] = jnp.ones((A_HEADS, DA_VROWS - A_VDIM, ts), BF16)
        vb_ref[:, :HEAD_DIM, tok] = group_t(5).reshape(B_HEADS, HEAD_DIM, ts).astype(BF16)
        vb_ref[:, HEAD_DIM:, tok] = jnp.ones((B_HEADS, NA_VROWS - HEAD_DIM, ts), BF16)


def _proj(x, mod, g, w_in_t, gains, cos_t, sin_t, tm):
    b, s, d = x.shape
    tok = lambda bi, i: (bi, 0, 0, i)
    seq = lambda bi, i: (bi, 0, i, 0)
    return pl.pallas_call(
        _proj_kernel,
        grid=(b, s // tm),
        in_specs=[
            pl.BlockSpec((None, tm, d), lambda bi, i: (bi, i, 0)),
            pl.BlockSpec((None, N_MOD, d), lambda bi, i: (bi, 0, 0)),
            pl.BlockSpec((1, d), lambda bi, i: (0, 0)),
            _resident(w_in_t.shape, lambda bi, i: (0, 0)),
            pl.BlockSpec((4, HEAD_DIM, tm), lambda bi, i: (0, 0, 0)),
            pl.BlockSpec((HEAD_DIM // 2, tm), lambda bi, i: (0, i)),
            pl.BlockSpec((HEAD_DIM // 2, tm), lambda bi, i: (0, i)),
        ],
        out_specs=[
            pl.BlockSpec((None, A_HEADS, A_VDIM, tm), tok),
            pl.BlockSpec((None, A_HEADS, tm, A_VDIM), seq),
            pl.BlockSpec((None, A_HEADS, DA_VROWS, tm), tok),
            pl.BlockSpec((None, B_HEADS, HEAD_DIM, tm), tok),
            pl.BlockSpec((None, B_HEADS // 2, tm, 2 * HEAD_DIM), seq),
            pl.BlockSpec((None, B_HEADS, NA_VROWS, tm), tok),
        ],
        out_shape=[
            jax.ShapeDtypeStruct((b, A_HEADS, A_VDIM, s), BF16),
            jax.ShapeDtypeStruct((b, A_HEADS, s, A_VDIM), BF16),
            jax.ShapeDtypeStruct((b, A_HEADS, DA_VROWS, s), BF16),
            jax.ShapeDtypeStruct((b, B_HEADS, HEAD_DIM, s), BF16),
            jax.ShapeDtypeStruct((b, B_HEADS // 2, s, 2 * HEAD_DIM), BF16),
            jax.ShapeDtypeStruct((b, B_HEADS, NA_VROWS, s), BF16),
        ],
        scratch_shapes=[pltpu.VMEM((2, PROJ_TOKEN_SUB if tm % PROJ_TOKEN_SUB == 0 else tm, d), BF16)],
        compiler_params=_params("parallel", "parallel"),
        name="qkv_proj",
    )(x, mod, g.reshape(1, d), w_in_t, gains, cos_t, sin_t)


def _dattn_kernel(q_ref, k_ref, v_ref, lam_ref, gn_ref, sg_ref, w13f_ref, w2f_ref, o_ref, w13b_ref, w2b_ref,
                  acc_ref, *, kblk, lambda_init):
    s_len = k_ref.shape[0]
    n_strips = q_ref.shape[1] // DA_QSUB
    lp = lam_ref[...]
    lam = (jnp.exp(jnp.sum(lp[0:1] * lp[1:2], axis=-1, keepdims=True))
           - jnp.exp(jnp.sum(lp[2:3] * lp[3:4], axis=-1, keepdims=True)) + lambda_init)
    gmax = jnp.max(jnp.abs(gn_ref[...]), axis=-1, keepdims=True)
    shift = (DA_BOUND_MARGIN * HEAD_DIM * QK_SCALE) * gmax[0:1] * gmax[1:2]
    z = jnp.zeros((HEAD_DIM, DA_QSUB), BF16)

    def strip_rhs(sb):
        qt = q_ref[:, sb * DA_QSUB:(sb + 1) * DA_QSUB]
        return jnp.concatenate([jnp.concatenate([qt[:HEAD_DIM], z], axis=0),
                                jnp.concatenate([z, qt[HEAD_DIM:]], axis=0)], axis=1)

    def finish(sb, num, den):
        o = num / den
        ot = o[:, :DA_QSUB] - lam * o[:, DA_QSUB:]
        ms = jnp.mean(ot * ot, axis=0, keepdims=True)
        y = ((ot * lax.rsqrt(ms + EPS)) * sg_ref[...]) * (1.0 - lambda_init)
        o_ref[sb * DA_QSUB:(sb + 1) * DA_QSUB, :] = y.T.astype(BF16)

    min_sum = None
    for sb in range(n_strips):
        if sb == n_strips // 2:
            w13b_ref[...] = w13f_ref[...].astype(BF16)
            w2b_ref[...] = w2f_ref[...].astype(BF16)
        rhs = strip_rhs(sb)
        num = None
        den8 = None
        for k0 in range(0, s_len, kblk):
            st = jnp.dot(k_ref[k0:k0 + kblk, :], rhs, preferred_element_type=F32)
            p = jnp.exp(st - shift)
            part = jnp.sum(p.reshape(kblk // 8, 8, 2 * DA_QSUB), axis=0)
            pv = jnp.dot(v_ref[:A_VDIM, k0:k0 + kblk], p.astype(BF16), preferred_element_type=F32)
            num = pv if num is None else num + pv
            den8 = part if den8 is None else den8 + part
        den = jnp.sum(den8, axis=0, keepdims=True)
        finish(sb, num, den)
        min_sum = den if min_sum is None else jnp.minimum(min_sum, den)

    sums_ok = jnp.min(min_sum) >= DA_MIN_SUM

    @pl.when(jnp.logical_not(sums_ok))
    def _():
        for sb in range(n_strips):
            rhs = strip_rhs(sb)
            acc_ref[...] = jnp.zeros_like(acc_ref)

            def safe_body(t, m):
                k0 = pl.multiple_of(t * DA_SAFE_KBLK, DA_SAFE_KBLK)
                st = jnp.dot(k_ref[pl.ds(k0, DA_SAFE_KBLK), :], rhs, preferred_element_type=F32)
                m_new = jnp.maximum(m, jnp.max(st, axis=0, keepdims=True))
                pv = jnp.dot(v_ref[:, pl.ds(k0, DA_SAFE_KBLK)], jnp.exp(st - m_new).astype(BF16),
                             preferred_element_type=F32)
                acc_ref[...] = acc_ref[...] * jnp.exp(m - m_new) + pv
                return m_new

            lax.fori_loop(0, s_len // DA_SAFE_KBLK, safe_body, jnp.full((1, 2 * DA_QSUB), NEG_INF, F32))
            finish(sb, acc_ref[:A_VDIM, :], acc_ref[A_VDIM:A_VDIM + 1, :])


def _cast_rows(total, steps):
    return min(r for r in range(BF16_SUBLANES, total + 1, BF16_SUBLANES) if total % r == 0 and total // r <= steps)


def _dattn(qa_t, ka, va_t, a_lambda, a_qk_gain, a_subln, lambda_init, qblk, ff_w13, ff_w2):
    b, nh, _, s = qa_t.shape
    assert qblk % DA_QSUB == 0
    nq = s // qblk
    depth, halves, d, f2 = ff_w13.shape
    n_sets = depth * halves - 1
    per_set = (b * nh * nq) // n_sets
    assert per_set >= 1
    rows13, rows2 = _cast_rows(d, per_set), _cast_rows(f2 // 2, per_set)

    def cast_block(rows_total, rows):
        nblk = rows_total // rows

        def locate(bi, h, i):
            blk = jnp.minimum((bi * nh + h) * nq + i, n_sets * nblk - 1)
            return blk // nblk, blk % nblk

        def src(bi, h, i):
            st, rb = locate(bi, h, i)
            return ((st + 1) // halves, (st + 1) % halves, rb, 0)

        def dst(bi, h, i):
            st, rb = locate(bi, h, i)
            return (st, rb, 0)

        return src, dst

    src13, dst13 = cast_block(d, rows13)
    src2, dst2 = cast_block(f2 // 2, rows2)
    return pl.pallas_call(
        functools.partial(_dattn_kernel, kblk=_tile(s, DA_KBLK), lambda_init=lambda_init),
        grid=(b, nh, nq),
        in_specs=[
            pl.BlockSpec((None, None, A_VDIM, qblk), lambda bi, h, i: (bi, h, 0, i)),
            pl.BlockSpec((None, None, s, A_VDIM), lambda bi, h, i: (bi, h, 0, 0)),
            pl.BlockSpec((None, None, DA_VROWS, s), lambda bi, h, i: (bi, h, 0, 0)),
            pl.BlockSpec((4, HEAD_DIM), lambda bi, h, i: (0, 0)),
            pl.BlockSpec((2, HEAD_DIM), lambda bi, h, i: (0, 0)),
            pl.BlockSpec((A_VDIM, DA_QSUB), lambda bi, h, i: (0, 0)),
            pl.BlockSpec((None, None, rows13, f2), src13),
            pl.BlockSpec((None, None, rows2, d), src2),
        ],
        out_specs=[
            pl.BlockSpec((None, qblk, A_VDIM), lambda bi, h, i: (bi, i, h)),
            pl.BlockSpec((None, rows13, f2), dst13),
            pl.BlockSpec((None, rows2, d), dst2),
        ],
        out_shape=[
            jax.ShapeDtypeStruct((b, s, nh * A_VDIM), BF16),
            jax.ShapeDtypeStruct((n_sets, d, f2), BF16),
            jax.ShapeDtypeStruct((n_sets, f2 // 2, d), BF16),
        ],
        scratch_shapes=[pltpu.VMEM((DA_VROWS, 2 * DA_QSUB), F32)],
        compiler_params=_params("arbitrary", "arbitrary", "arbitrary"),
        name="diff_attn",
    )(qa_t, ka, va_t, a_lambda, a_qk_gain, jnp.broadcast_to(a_subln[:, None], (A_VDIM, DA_QSUB)), ff_w13, ff_w2)


def _toeplitz_selector():
    kc = np.arange(GRID_W)[:, None]
    qc = np.arange(GRID_W)[None, :]
    cs = np.clip(qc - NA_COLS // 2, 0, GRID_W - NA_COLS)
    col_ok = (kc >= cs) & (kc < cs + NA_COLS)
    dc = np.clip(kc - qc + NA_COLS - 1, 0, 2 * NA_COLS - 2)
    sel = (np.arange(2 * NA_COLS)[:, None, None] == dc[None]) & col_ok[None]
    mask = np.where(col_ok, 0.0, NEG_INF)
    return (sel.reshape(2 * NA_COLS, GRID_W * GRID_W).astype(np.float32),
            mask.reshape(1, GRID_W * GRID_W).astype(np.float32))


def _rpb_expand_kernel(r_ref, gn_ref, sel_ref, mask_ref, o_ref, *, ndr, ndc):
    r = r_ref[...]
    nh = r.shape[0] // NA_DR_PAD
    rows = lax.broadcasted_iota(jnp.int32, r.shape, 0)
    cols = lax.broadcasted_iota(jnp.int32, r.shape, 1)
    real = ((rows & (NA_DR_PAD - 1)) < ndr) & (cols < ndc)
    row_max = jnp.max(jnp.where(real, r, NEG_INF), axis=1, keepdims=True)
    head_max = jnp.max(row_max.reshape(nh, NA_DR_PAD, 1), axis=1, keepdims=True)
    gmax = jnp.max(jnp.abs(gn_ref[...]), axis=-1, keepdims=True)
    bound = (DA_BOUND_MARGIN * HEAD_DIM * QK_SCALE) * gmax[0:1] * gmax[1:2]
    shift = jnp.broadcast_to(head_max + bound[None], (nh, NA_DR_PAD, 1)).reshape(r.shape[0], 1)
    o_ref[...] = (jnp.dot(r, sel_ref[...], preferred_element_type=F32, precision=lax.Precision.HIGHEST)
                  + mask_ref[...]) - shift


def _na_bias_table(rpb, qk_gain):
    nh, ndr, ndc = rpb.shape
    sel, mask = _toeplitz_selector()
    r = jnp.pad(rpb, ((0, 0), (0, NA_DR_PAD - ndr), (0, sel.shape[0] - ndc))).reshape(nh * NA_DR_PAD, sel.shape[0])
    flat = pl.pallas_call(
        functools.partial(_rpb_expand_kernel, ndr=ndr, ndc=ndc),
        out_shape=jax.ShapeDtypeStruct((nh * NA_DR_PAD, GRID_W * GRID_W), F32),
        name="rpb_expand",
    )(r, qk_gain, sel, mask)
    tile = flat.reshape(nh, NA_DR_PAD, GRID_W, GRID_W)
    neg = jnp.full((nh, GRID_W, GRID_W), NEG_INF, F32)
    half = NA_ROWS // 2
    kinds = []
    for kind in range(3):
        win_rows = []
        for i in range(NA_WIN_ROWS):
            blocks = []
            for j in range(NA_BLOCK_ROWS):
                if kind == 0:
                    dr, ok = i - j + NA_ROWS - 1, i < NA_ROWS
                elif kind == 1:
                    dr, ok = i - j + half - 1, j <= i < j + NA_ROWS
                else:
                    dr, ok = i - j - 1, i >= NA_WIN_ROWS - NA_ROWS
                blocks.append(tile[:, dr] if ok else neg)
            win_rows.append(jnp.concatenate(blocks, axis=-1))
        kinds.append(jnp.concatenate(win_rows, axis=1))
    return jnp.stack(kinds)


def _na_kernel(q_ref, k_ref, v_ref, bias_ref, o_ref, *, nblk):
    qn = NA_BLOCK_ROWS * GRID_W
    win = NA_WIN_ROWS * GRID_W
    nh = q_ref.shape[0]
    per_step = q_ref.shape[2] // qn
    z = jnp.zeros((HEAD_DIM, qn), BF16)

    def block(j):
        rb = pl.program_id(2) * per_step + j
        t0 = pl.multiple_of(jnp.clip(rb - 1, 0, nblk - 3) * qn, qn)
        kind = jnp.where(rb == 0, 0, jnp.where(rb == nblk - 1, 2, 1))

        def logits(h):
            qt = q_ref[h, :, j * qn:(j + 1) * qn]
            rhs = jnp.concatenate([qt, z] if h % 2 == 0 else [z, qt], axis=0)
            kwin = k_ref[h // 2, pl.ds(t0, win), :]
            return jnp.dot(kwin, rhs, preferred_element_type=F32) + bias_ref[kind, h]

        def weighted(h, p):
            return jnp.dot(v_ref[h, :, pl.ds(t0, win)], p.astype(BF16), preferred_element_type=F32)

        return logits, weighted

    def store(j, pvs):
        outs = [pv[:HEAD_DIM] / pv[HEAD_DIM:HEAD_DIM + 1] for pv in pvs]
        o_ref[j * qn:(j + 1) * qn, :] = jnp.concatenate(outs, axis=0).T.astype(BF16)

    sums = []
    for j in range(per_step):
        logits, weighted = block(j)
        sts = [logits(h) for h in range(nh)]
        pvs = [weighted(h, jnp.exp(st)) for h, st in enumerate(sts)]
        store(j, pvs)
        sums += [pv[HEAD_DIM:HEAD_DIM + 1] for pv in pvs]

    sums_ok = jnp.min(functools.reduce(jnp.minimum, sums)) >= NA_MIN_SUM

    @pl.when(jnp.logical_not(sums_ok))
    def _():
        for j in range(per_step):
            logits, weighted = block(j)
            safe = []
            for h in range(nh):
                st = logits(h)
                safe.append(weighted(h, jnp.exp(st - jnp.max(st, axis=0, keepdims=True))))
            store(j, safe)


def _na(qb_t, kb, vb_t, bias_tab):
    b, nh, _, s = qb_t.shape
    qn = NA_BLOCK_ROWS * GRID_W
    nblk = s // qn
    assert nblk >= 3
    per_step = NA_STEP_BLOCKS if nblk % NA_STEP_BLOCKS == 0 else 1
    qn *= per_step
    groups = NA_HEAD_GROUPS if nh % (2 * NA_HEAD_GROUPS) == 0 else 1
    hg = nh // groups
    return pl.pallas_call(
        functools.partial(_na_kernel, nblk=nblk),
        grid=(b, groups, nblk // per_step),
        in_specs=[
            pl.BlockSpec((None, hg, HEAD_DIM, qn), lambda bi, g, rb: (bi, g, 0, rb)),
            pl.BlockSpec((None, hg // 2, s, 2 * HEAD_DIM), lambda bi, g, rb: (bi, g, 0, 0)),
            pl.BlockSpec((None, hg, NA_VROWS, s), lambda bi, g, rb: (bi, g, 0, 0)),
            pl.BlockSpec((bias_tab.shape[0], hg) + bias_tab.shape[2:], lambda bi, g, rb: (0, g, 0, 0)),
        ],
        out_specs=pl.BlockSpec((None, qn, hg * HEAD_DIM), lambda bi, g, rb: (bi, rb, g)),
        out_shape=jax.ShapeDtypeStruct((b, s, nh * HEAD_DIM), BF16),
        compiler_params=_params("parallel", "arbitrary", "arbitrary"),
        name="nbr_attn",
    )(qb_t, kb, vb_t, bias_tab)


def _tile(s, want):
    return want if s % want == 0 else s


def kernel(x, c, ada_w, ada_b, norm_g, ff_w13, ff_w2, w_in, w_out, a_qk_norm, a_lambda, a_subln,
           b_qk_norm, b_rpb, pool_w, pool_b, pool_scale):
    b, s, d = x.shape
    depth = ada_w.shape[0]
    tm = _tile(s, TOKEN_TILE)
    ffn_tm = _tile(s, FFN_TOKEN_TILE)
    mod = _adaln(c, ada_w, ada_b)

    pos = jnp.arange(s, dtype=F32)
    inv = ROPE_THETA ** (-jnp.arange(0, HEAD_DIM, 2, dtype=F32) / HEAD_DIM)
    ang = inv[:, None] * pos[None, :]
    cos_t, sin_t = jnp.cos(ang), jnp.sin(ang)

    first = (ff_w13[0, :1].astype(BF16), ff_w2[0, :1].astype(BF16))
    rest = None

    def ffn_weights(layer, half):
        idx = 2 * layer + half
        return (*first, 0) if idx == 0 else (*rest, idx - 1)

    for layer in range(depth):
        m = mod[layer]
        x = _ffn(x, m, norm_g[layer, 0], *ffn_weights(layer, 0), ffn_tm)
        if layer % 2 == 0:
            e = layer // 2
            lambda_init = 0.8 - 0.6 * math.exp(-0.3 * layer)
            gains = jnp.stack([a_qk_norm[e, 0], a_qk_norm[e, 1], b_qk_norm[e, 0], b_qk_norm[e, 1]])
            gains = jnp.broadcast_to(gains[:, :, None], (4, HEAD_DIM, tm))
            qa_t, ka, va_t, qb_t, kb, vb_t = _proj(x, m, norm_g[layer, 1], w_in[e].T.astype(BF16),
                                                   gains, cos_t, sin_t, tm)
            mix_a, w13_rest, w2_rest = _dattn(qa_t, ka, va_t, a_lambda[e], a_qk_norm[e], a_subln[e], lambda_init,
                                              _tile(s, DA_QBLK), ff_w13, ff_w2)
            if rest is None:
                rest = (w13_rest, w2_rest)
            mix_b = _na(qb_t, kb, vb_t, _na_bias_table(b_rpb[e], b_qk_norm[e]))
            x = _attn_mix_ffn(x, mix_a, mix_b, w_out[e].astype(BF16), m, norm_g[layer, 2],
                              *ffn_weights(layer, 1), ffn_tm)
        else:
            o = layer // 2
            x = _pool_mix_ffn(x, norm_g[layer, 1], pool_w[o].astype(BF16), pool_b[o], pool_scale[o],
                              m, norm_g[layer, 2], *ffn_weights(layer, 1), ffn_tm)
    return x
```
